```python
import math
import jax, jax.numpy as jnp
from jax import lax
import numpy as np

D_MODEL = 4096
BATCH = 4
SEQ = 2048
DEPTH = 2

N_BRANCHES = 3
BRANCH_WIDTH = 1024
GLA_HEADS = 4
GLA_DK = 128
GLA_DV = 256
GLA_RANK = 16
GLA_TAU = 16.0
GLA_CHUNK = 64
DSA_HEADS = 8
DSA_DH = 128
IDX_HEADS = 32
IDX_DIM = 64
TOPK_MAX = 256
SB_HEADS = 8
SB_DH = 128
Q_BLOCK = 128
N_BUCKETS = 32
MAX_DISTANCE = 128
D_FF = 4 * D_MODEL
EPS = 1e-6

IN_WIDTHS = (
    GLA_HEADS * GLA_DK,
    GLA_HEADS * GLA_DK,
    GLA_HEADS * GLA_DV,
    GLA_HEADS * GLA_DV,
    GLA_RANK,
    DSA_HEADS * DSA_DH,
    DSA_DH,
    DSA_DH,
    IDX_HEADS * IDX_DIM,
    IDX_DIM,
    IDX_HEADS,
    SB_HEADS * SB_DH,
    SB_HEADS * SB_DH,
    SB_HEADS * SB_DH,
    N_BRANCHES * D_MODEL,
)
IN_COLS = sum(IN_WIDTHS)

kernel_name = "hybrid_gla_dsa_stickbreaking_gated"


def rmsnorm(x, g):
    xf = x.astype(jnp.float32)
    y = xf * lax.rsqrt(jnp.mean(xf * xf, axis=-1, keepdims=True) + EPS)
    return (y * g.astype(jnp.float32)).astype(x.dtype)


def split_projection(proj):
    points, acc = [], 0
    for w in IN_WIDTHS[:-1]:
        acc += w
        points.append(acc)
    return jnp.split(proj, points, axis=-1)


def rel_bucket(dist):
    max_exact = N_BUCKETS // 2
    d = jnp.maximum(dist, 1).astype(jnp.float32)
    large = max_exact + (jnp.log(d / max_exact) / math.log(MAX_DISTANCE / max_exact)
                         * (N_BUCKETS - max_exact)).astype(jnp.int32)
    large = jnp.minimum(large, N_BUCKETS - 1)
    return jnp.where(dist < max_exact, dist, large)


def gla_branch(q, k, v, g_out, a_low, gate_up, gate_bias, head_gain):
    B, T, _ = q.shape
    C = GLA_CHUNK
    N = T // C
    f32 = jnp.float32
    log_a = jax.nn.log_sigmoid((a_low @ gate_up + gate_bias).astype(f32)) / GLA_TAU

    def heads(t, d):
        return t.astype(f32).reshape(B, N, C, GLA_HEADS, d).transpose(1, 0, 3, 2, 4)

    qc = heads(q, GLA_DK) * (GLA_DK ** -0.5)
    kc = heads(k, GLA_DK)
    vc = heads(v, GLA_DV)
    gc = heads(log_a, GLA_DK)
    causal = jnp.tril(jnp.ones((C, C), dtype=bool))

    def step(S, inp):
        qi, ki, vi, gi = inp
        b = jnp.cumsum(gi, axis=-2)
        o_inter = jnp.einsum('bhcd,bhde->bhce', qi * jnp.exp(b), S)
        diff = b[:, :, :, None, :] - b[:, :, None, :, :]
        decay = jnp.exp(jnp.where(causal[:, :, None], diff, -jnp.inf))
        A = jnp.einsum('bhid,bhjd,bhijd->bhij', qi, ki, decay)
        o = o_inter + jnp.einsum('bhij,bhje->bhie', A, vi)
        b_last = b[:, :, -1:, :]
        S = jnp.exp(b_last[:, :, 0, :, None]) * S + jnp.einsum(
            'bhcd,bhce->bhde', ki * jnp.exp(b_last - b), vi)
        return S, o

    S0 = jnp.zeros((B, GLA_HEADS, GLA_DK, GLA_DV), f32)
    _, o = lax.scan(step, S0, (qc, kc, vc, gc))
    o = o.transpose(1, 0, 3, 2, 4).reshape(B, T, GLA_HEADS, GLA_DV)
    o = rmsnorm(o, head_gain).reshape(B, T, GLA_HEADS * GLA_DV)
    o = o * jax.nn.silu(g_out.astype(f32))
    return o.astype(q.dtype)


def dsa_branch(q, k, v, q_idx, k_idx, w_idx, rel_bias):
    B, T, _ = q.shape
    f32 = jnp.float32
    topk = min(TOPK_MAX, T // 4)
    nb = T // Q_BLOCK
    q = q.reshape(B, T, DSA_HEADS, DSA_DH)
    q_idx = q_idx.reshape(B, T, IDX_HEADS, IDX_DIM)
    w_idx = w_idx * (IDX_HEADS ** -0.5)
    key_pos = jnp.arange(T, dtype=jnp.int32)
    gather = jax.vmap(lambda kv, ii: kv[ii])

    def block(start):
        qb = lax.dynamic_slice_in_dim(q, start, Q_BLOCK, axis=1)
        qib = lax.dynamic_slice_in_dim(q_idx, start, Q_BLOCK, axis=1)
        wb = lax.dynamic_slice_in_dim(w_idx, start, Q_BLOCK, axis=1)
        qpos = start + jnp.arange(Q_BLOCK, dtype=jnp.int32)
        s_idx = jnp.einsum('bqhd,bsd->bqhs', qib, k_idx) * (IDX_DIM ** -0.5)
        I = jnp.einsum('bqh,bqhs->bqs', wb, jax.nn.relu(s_idx)).astype(f32)
        visible = key_pos[None, :] <= qpos[:, None]
        I = jnp.where(visible[None], I, -jnp.inf)
        _, sel = lax.top_k(I, topk)
        k_sel = gather(k, sel)
        v_sel = gather(v, sel)
        dist = qpos[None, :, None] - sel
        bias = rel_bias[rel_bucket(jnp.maximum(dist, 0))]
        logits = (jnp.einsum('bqhd,bqkd->bqhk', qb, k_sel).astype(f32) * (DSA_DH ** -0.5)
                  + bias.astype(f32).transpose(0, 1, 3, 2))
        logits = jnp.where((dist >= 0)[:, :, None, :], logits, -jnp.inf)
        p = jax.nn.softmax(logits, axis=-1)
        o = jnp.einsum('bqhk,bqkd->bqhd', p.astype(v.dtype), v_sel)
        return o.reshape(B, Q_BLOCK, DSA_HEADS * DSA_DH)

    out = lax.map(block, jnp.arange(nb, dtype=jnp.int32) * Q_BLOCK)
    return out.transpose(1, 0, 2, 3).reshape(B, T, DSA_HEADS * DSA_DH)


def stickbreaking_branch(q, k, v):
    B, T, _ = q.shape
    f32 = jnp.float32
    nb = T // Q_BLOCK
    q = q.reshape(B, T, SB_HEADS, SB_DH)
    k = k.reshape(B, T, SB_HEADS, SB_DH)
    v = v.reshape(B, T, SB_HEADS, SB_DH)
    key_pos = jnp.arange(T, dtype=jnp.int32)

    def block(start):
        qb = lax.dynamic_slice_in_dim(q, start, Q_BLOCK, axis=1)
        qpos = start + jnp.arange(Q_BLOCK, dtype=jnp.int32)
        z = jnp.einsum('bqhd,bshd->bhqs', qb, k).astype(f32) * (SB_DH ** -0.5)
        strict = key_pos[None, :] < qpos[:, None]
        log_beta = jax.nn.log_sigmoid(z)
        log_1m = jnp.where(strict, jax.nn.log_sigmoid(-z), 0.0)
        after = lax.cumsum(log_1m, axis=3, reverse=True) - log_1m
        w = jnp.where(strict, jnp.exp(log_beta + after), 0.0)
        o = jnp.einsum('bhqs,bshd->bqhd', w.astype(v.dtype), v)
        return o.reshape(B, Q_BLOCK, SB_HEADS * SB_DH)

    out = lax.map(block, jnp.arange(nb, dtype=jnp.int32) * Q_BLOCK)
    return out.transpose(1, 0, 2, 3).reshape(B, T, SB_HEADS * SB_DH)


def mixer_block(h, w_in, gla_gate_up, gla_gate_bias, gla_head_gain, rel_bias, w_branch, w_out):
    B, T, _ = h.shape
    (gq, gk, gv, gg, ga, dq, dk, dv, iq, ik, iw, sq, sk, sv, gates) = split_projection(h @ w_in)
    o_a = gla_branch(gq, gk, gv, gg, ga, gla_gate_up, gla_gate_bias, gla_head_gain)
    o_b = dsa_branch(dq, dk, dv, iq, ik, iw, rel_bias)
    o_c = stickbreaking_branch(sq, sk, sv)
    gates = jax.nn.sigmoid(gates.reshape(B, T, N_BRANCHES, D_MODEL))
    merged = (gates[:, :, 0] * (o_a @ w_branch[0])
              + gates[:, :, 1] * (o_b @ w_branch[1])
              + gates[:, :, 2] * (o_c @ w_branch[2]))
    return merged @ w_out


def sq_relu_mlp(h, w_up, w_down):
    return jnp.square(jax.nn.relu(h @ w_up)) @ w_down


def setup_inputs(seed: int = 0) -> dict:
    key = jax.random.key(seed)
    ks = jax.random.split(key, 16)
    f32 = jnp.float32

    def nrm(k, shape, fan_in):
        return jax.random.normal(k, shape, f32) * (fan_in ** -0.5)

    def gain(k, shape):
        return 1.0 + 0.05 * jax.random.normal(k, shape, f32)

    return {
        "x": jax.random.normal(ks[0], (BATCH, SEQ, D_MODEL), f32),
        "rel_bias": 0.5 * jax.random.normal(ks[1], (N_BUCKETS, DSA_HEADS), f32),
        "norm_mix_pre": gain(ks[2], (DEPTH, D_MODEL)),
        "norm_mix_post": gain(ks[3], (DEPTH, D_MODEL)),
        "norm_mlp_pre": gain(ks[4], (DEPTH, D_MODEL)),
        "norm_mlp_post": gain(ks[5], (DEPTH, D_MODEL)),
        "w_in": nrm(ks[6], (DEPTH, D_MODEL, IN_COLS), D_MODEL),
        "gla_gate_up": nrm(ks[7], (DEPTH, GLA_RANK, GLA_HEADS * GLA_DK), GLA_RANK),
        "gla_gate_bias": 0.1 * jax.random.normal(ks[8], (DEPTH, GLA_HEADS * GLA_DK), f32),
        "gla_head_gain": gain(ks[9], (DEPTH, GLA_DV)),
        "w_branch": nrm(ks[10], (DEPTH, N_BRANCHES, BRANCH_WIDTH, D_MODEL), BRANCH_WIDTH),
        "w_out": nrm(ks[11], (DEPTH, D_MODEL, D_MODEL), D_MODEL),
        "w_mlp_up": nrm(ks[12], (DEPTH, D_MODEL, D_FF), D_MODEL),
        "w_mlp_down": nrm(ks[13], (DEPTH, D_FF, D_MODEL), D_FF),
    }


def reference(x, rel_bias, norm_mix_pre, norm_mix_post, norm_mlp_pre, norm_mlp_post,
              w_in, gla_gate_up, gla_gate_bias, gla_head_gain, w_branch, w_out,
              w_mlp_up, w_mlp_down):
    for l in range(DEPTH):
        h = mixer_block(rmsnorm(x, norm_mix_pre[l]), w_in[l], gla_gate_up[l], gla_gate_bias[l],
                        gla_head_gain[l], rel_bias, w_branch[l], w_out[l])
        x = x + rmsnorm(h, norm_mix_post[l])
        h = sq_relu_mlp(rmsnorm(x, norm_mlp_pre[l]), w_mlp_up[l], w_mlp_down[l])
        x = x + rmsnorm(h, norm_mlp_post[l])
    return x
```

```python
import functools
import math

import jax
import jax.numpy as jnp
from jax import lax
from jax.experimental import pallas as pl
from jax.experimental.pallas import tpu as pltpu

F32 = jnp.float32
BF16 = jnp.bfloat16

D_MODEL = 4096
DEPTH = 2
N_BRANCHES = 3
BRANCH_WIDTH = 1024
GLA_HEADS, GLA_DK, GLA_DV, GLA_RANK, GLA_TAU, GLA_CHUNK = 4, 128, 256, 16, 16.0, 64
GLA_SUB = 16
DSA_HEADS, DSA_DH, IDX_HEADS, IDX_DIM, TOPK_MAX = 8, 128, 32, 64, 256
SB_HEADS, SB_DH = 8, 128
Q_BLOCK = 128
N_BUCKETS, MAX_DISTANCE = 32, 128
D_FF = 4 * D_MODEL
EPS = 1e-6

_SRC = {}
_acc = 0
for _name, _w in (("gq", 512), ("gk", 512), ("gv", 1024), ("gg", 1024), ("ga", 16), ("dq", 1024),
                  ("dk", 128), ("dv", 128), ("iq", 2048), ("ik", 64), ("iw", 32), ("sq", 1024),
                  ("sk", 1024), ("sv", 1024), ("gates", 3 * D_MODEL)):
    _SRC[_name] = (_acc, _w)
    _acc += _w
IN_COLS = _acc

_ORDER = ("gates", "gq", "gk", "gv", "gg", "sq", "sk", "sv", "iq", "dq", "dk", "dv", "ga", "ik", "iw")
_DST = {}
_acc = 0
for _name in _ORDER:
    _DST[_name] = _acc
    _acc += _SRC[_name][1]
SMALL_OFF = _DST["ga"]
GA_LO, IK_LO, IW_LO = 0, GLA_RANK, GLA_RANK + IDX_DIM
LANE = 128
NP_COLS = SMALL_OFF + LANE

VMEM_LIMIT = 56 * 1024 * 1024


def _cparams(sem):
    return pltpu.CompilerParams(dimension_semantics=sem, vmem_limit_bytes=VMEM_LIMIT)


def _rms_kernel(x_ref, g_ref, o_ref):
    x = x_ref[...]
    y = x * lax.rsqrt(jnp.mean(x * x, axis=-1, keepdims=True) + EPS)
    o_ref[...] = (y * g_ref[...]).astype(o_ref.dtype)


def rmsnorm_cast(x, g, tm=256):
    m, d = x.shape
    return pl.pallas_call(
        _rms_kernel,
        grid=(m // tm,),
        in_specs=[pl.BlockSpec((tm, d), lambda i: (i, 0)), pl.BlockSpec((1, d), lambda i: (0, 0))],
        out_specs=pl.BlockSpec((tm, d), lambda i: (i, 0)),
        out_shape=jax.ShapeDtypeStruct((m, d), BF16),
        compiler_params=_cparams(("parallel",)),
        name="rmsnorm_cast",
    )(x, g.reshape(1, d))


def _post_kernel(y_ref, x_ref, gp_ref, gn_ref, xo_ref, ho_ref):
    y = y_ref[...]
    yn = y * lax.rsqrt(jnp.mean(y * y, axis=-1, keepdims=True) + EPS) * gp_ref[...]
    xn = x_ref[...] + yn
    xo_ref[...] = xn
    hn = xn * lax.rsqrt(jnp.mean(xn * xn, axis=-1, keepdims=True) + EPS) * gn_ref[...]
    ho_ref[...] = hn.astype(ho_ref.dtype)


def _post_last_kernel(y_ref, x_ref, gp_ref, xo_ref):
    y = y_ref[...]
    yn = y * lax.rsqrt(jnp.mean(y * y, axis=-1, keepdims=True) + EPS) * gp_ref[...]
    xo_ref[...] = x_ref[...] + yn


def post_norm_residual(y, x, g_post, g_next, tm=256):
    m, d = x.shape
    row = pl.BlockSpec((tm, d), lambda i: (i, 0))
    vec = pl.BlockSpec((1, d), lambda i: (0, 0))
    if g_next is None:
        return pl.pallas_call(
            _post_last_kernel, grid=(m // tm,), in_specs=[row, row, vec], out_specs=row,
            out_shape=jax.ShapeDtypeStruct((m, d), F32),
            compiler_params=_cparams(("parallel",)), name="post_last",
        )(y, x, g_post.reshape(1, d)), None
    return pl.pallas_call(
        _post_kernel, grid=(m // tm,), in_specs=[row, row, vec, vec], out_specs=(row, row),
        out_shape=(jax.ShapeDtypeStruct((m, d), F32), jax.ShapeDtypeStruct((m, d), BF16)),
        compiler_params=_cparams(("parallel",)), name="post_norm",
    )(y, x, g_post.reshape(1, d), g_next.reshape(1, d))


def _mm_kernel(x_ref, w_ref, o_ref, *, sq_relu):
    r = jnp.dot(x_ref[...], w_ref[...], preferred_element_type=F32)
    if sq_relu:
        r = jnp.square(jnp.maximum(r, 0.0))
    o_ref[...] = r.astype(o_ref.dtype)


def _mm_acc_kernel(x_ref, w_ref, o_ref, acc_ref, *, nk):
    k = pl.program_id(2)

    @pl.when(k == 0)
    def _():
        acc_ref[...] = jnp.zeros_like(acc_ref)

    acc_ref[...] += jnp.dot(x_ref[...], w_ref[...], preferred_element_type=F32)

    @pl.when(k == nk - 1)
    def _():
        o_ref[...] = acc_ref[...].astype(o_ref.dtype)


def matmul(x, w, out_dtype, tm, tn, tk=None, sq_relu=False, name="matmul"):
    m, kd = x.shape
    _, n = w.shape
    if tk is None or tk == kd:
        return pl.pallas_call(
            functools.partial(_mm_kernel, sq_relu=sq_relu),
            grid=(m // tm, n // tn),
            in_specs=[pl.BlockSpec((tm, kd), lambda i, j: (i, 0)),
                      pl.BlockSpec((kd, tn), lambda i, j: (0, j))],
            out_specs=pl.BlockSpec((tm, tn), lambda i, j: (i, j)),
            out_shape=jax.ShapeDtypeStruct((m, n), out_dtype),
            compiler_params=_cparams(("parallel", "parallel")),
            name=name,
        )(x, w)
    nk = kd // tk
    return pl.pallas_call(
        functools.partial(_mm_acc_kernel, nk=nk),
        grid=(m // tm, n // tn, nk),
        in_specs=[pl.BlockSpec((tm, tk), lambda i, j, k: (i, k)),
                  pl.BlockSpec((tk, tn), lambda i, j, k: (k, j))],
        out_specs=pl.BlockSpec((tm, tn), lambda i, j, k: (i, j)),
        out_shape=jax.ShapeDtypeStruct((m, n), out_dtype),
        scratch_shapes=[pltpu.VMEM((tm, tn), F32)],
        compiler_params=_cparams(("parallel", "parallel", "arbitrary")),
        name=name,
    )(x, w)


def _split_bf16(x):
    hi = x.astype(BF16)
    lo = (x - hi.astype(F32)).astype(BF16)
    return hi, lo


def _log_sigmoid_pair(z):
    sp = jnp.log1p(jnp.exp(-jnp.abs(z)))
    return jnp.minimum(z, 0.0) - sp, -jnp.maximum(z, 0.0) - sp


def _dot_nt(a, b):
    return lax.dot_general(a, b, (((1,), (1,)), ((), ())), preferred_element_type=F32)


def _dot_tn(a, b):
    return lax.dot_general(a, b, (((0,), (0,)), ((), ())), preferred_element_type=F32)


def _sb_kernel(q_ref, k_ref, v_ref, o_ref, *, tq):
    qi = pl.program_id(2)
    scale = SB_DH ** -0.5
    q = q_ref[...].astype(BF16)
    rows = lax.broadcasted_iota(jnp.int32, (tq, tq), 0)
    cols = lax.broadcasted_iota(jnp.int32, (tq, tq), 1)
    later = (rows > cols).astype(BF16)
    strict = cols < rows

    def tile(j, carry, acc, diag):
        start = pl.multiple_of(j * tq, tq)
        kb = k_ref[pl.ds(start, tq), :].astype(BF16)
        vb = v_ref[pl.ds(start, tq), :].astype(BF16)
        z = _dot_nt(q, kb) * scale
        log_beta, log_1m = _log_sigmoid_pair(z)
        if diag:
            log_1m = jnp.where(strict, log_1m, 0.0)
        hi, lo = _split_bf16(log_1m)
        after = (jnp.dot(hi, later, preferred_element_type=F32)
                 + jnp.dot(lo, later, preferred_element_type=F32)) + carry
        w = jnp.exp(log_beta + after)
        if diag:
            w = jnp.where(strict, w, 0.0)
        acc = acc + jnp.dot(w.astype(BF16), vb, preferred_element_type=F32)
        carry = carry + jnp.sum(log_1m, axis=-1, keepdims=True)
        return carry, acc

    carry, acc = tile(qi, jnp.zeros((tq, 1), F32), jnp.zeros((tq, SB_DH), F32), True)

    def body(jj, c):
        return tile(qi - jj, c[0], c[1], False)

    carry, acc = lax.fori_loop(1, qi + 1, body, (carry, acc))
    o_ref[...] = acc.astype(o_ref.dtype)


def stickbreaking(p, batch, seq, tq=128):
    nq = seq // tq
    qc, kc, vc = (_DST[n] // SB_DH for n in ("sq", "sk", "sv"))
    return pl.pallas_call(
        functools.partial(_sb_kernel, tq=tq),
        grid=(batch, SB_HEADS, nq),
        in_specs=[pl.BlockSpec((tq, SB_DH), lambda b, h, i: (b * nq + i, qc + h)),
                  pl.BlockSpec((seq, SB_DH), lambda b, h, i: (b, kc + h)),
                  pl.BlockSpec((seq, SB_DH), lambda b, h, i: (b, vc + h))],
        out_specs=pl.BlockSpec((tq, SB_DH), lambda b, h, i: (b * nq + i, h)),
        out_shape=jax.ShapeDtypeStruct((batch * seq, SB_HEADS * SB_DH), BF16),
        compiler_params=_cparams(("parallel", "parallel", "arbitrary")),
        name="stickbreaking",
    )(p, p, p)


def _gla_kernel(q_ref, k_ref, v_ref, g_ref, small_ref, up_ref, bias_ref, gain_ref, o_ref, st_ref):
    c = GLA_CHUNK

    @pl.when(pl.program_id(1) == 0)
    def _():
        st_ref[...] = jnp.zeros_like(st_ref)

    a_hi, a_lo = _split_bf16(small_ref[:, GA_LO:GA_LO + GLA_RANK])
    u_hi, u_lo = _split_bf16(up_ref[...])
    x = (jnp.dot(a_hi, u_hi, preferred_element_type=F32) + jnp.dot(a_hi, u_lo, preferred_element_type=F32)
         + jnp.dot(a_lo, u_hi, preferred_element_type=F32)) + bias_ref[...]
    log_a = _log_sigmoid_pair(x)[0] / GLA_TAU
    r_i = lax.broadcasted_iota(jnp.int32, (c, c), 0)
    c_i = lax.broadcasted_iota(jnp.int32, (c, c), 1)
    tril = (c_i <= r_i).astype(BF16)
    g_hi, g_lo = _split_bf16(log_a)
    b_all = jnp.dot(tril, g_hi, preferred_element_type=F32) + jnp.dot(tril, g_lo, preferred_element_type=F32)

    sub = GLA_SUB
    lane = lax.broadcasted_iota(jnp.int32, (sub, LANE), 1)
    jrow = lax.broadcasted_iota(jnp.int32, (sub, LANE), 0)
    gain = gain_ref[...]

    for h in range(GLA_HEADS):
        b = b_all[:, h * GLA_DK:(h + 1) * GLA_DK]
        q = q_ref[:, h * GLA_DK:(h + 1) * GLA_DK] * (GLA_DK ** -0.5)
        k = k_ref[:, h * GLA_DK:(h + 1) * GLA_DK]
        v = v_ref[:, h * GLA_DV:(h + 1) * GLA_DV].astype(BF16)
        st = st_ref[h]
        o_inter = _dot_nt((q * jnp.exp(b)).astype(BF16), st.astype(BF16))

        o_rows = []
        for blk in range(c // sub):
            lo_r, hi_r = blk * sub, (blk + 1) * sub
            b_blk, q_blk, k_blk = b[lo_r:hi_r], q[lo_r:hi_r], k[lo_r:hi_r]
            o_blk = o_inter[lo_r:hi_r]
            if blk > 0:
                b_first = b[lo_r:lo_r + 1]
                qt = (q_blk * jnp.exp(b_blk - b_first)).astype(BF16)
                kt = (k[:lo_r] * jnp.exp(b_first - b[:lo_r])).astype(BF16)
                a_off = _dot_nt(qt, kt)
                o_blk = o_blk + jnp.dot(a_off.astype(BF16), v[:lo_r], preferred_element_type=F32)
            a_t = jnp.zeros((sub, LANE), F32)
            for i in range(sub):
                e = jnp.exp(jnp.minimum(b_blk[i:i + 1] - b_blk, 0.0))
                col = jnp.sum(q_blk[i:i + 1] * k_blk * e, axis=-1, keepdims=True)
                a_t = jnp.where(lane == i, col, a_t)
            a_t = jnp.where(jrow <= lane, a_t, 0.0)
            o_diag = _dot_tn(a_t.astype(BF16), v[lo_r:hi_r])
            o_rows.append(o_blk + o_diag[:sub])
        o = jnp.concatenate(o_rows, axis=0)

        b_last = b[c - 1:c]
        kd = (k * jnp.exp(b_last - b)).astype(BF16)
        st_ref[h] = st * jnp.exp(b_last) + _dot_tn(v, kd)

        o = o * lax.rsqrt(jnp.mean(o * o, axis=-1, keepdims=True) + EPS) * gain
        g = g_ref[:, h * GLA_DV:(h + 1) * GLA_DV]
        o = o * (g / (1.0 + jnp.exp(-g)))
        o_ref[:, h * GLA_DV:(h + 1) * GLA_DV] = o.astype(o_ref.dtype)


def gla(p, gate_up, gate_bias, head_gain, batch, seq):
    c = GLA_CHUNK
    n = seq // c
    wq, wv = GLA_HEADS * GLA_DK, GLA_HEADS * GLA_DV
    return pl.pallas_call(
        _gla_kernel,
        grid=(batch, n),
        in_specs=[pl.BlockSpec((c, wq), lambda b, i: (b * n + i, _DST["gq"] // wq)),
                  pl.BlockSpec((c, wq), lambda b, i: (b * n + i, _DST["gk"] // wq)),
                  pl.BlockSpec((c, wv), lambda b, i: (b * n + i, _DST["gv"] // wv)),
                  pl.BlockSpec((c, wv), lambda b, i: (b * n + i, _DST["gg"] // wv)),
                  pl.BlockSpec((c, LANE), lambda b, i: (b * n + i, SMALL_OFF // LANE)),
                  pl.BlockSpec((GLA_RANK, wq), lambda b, i: (0, 0)),
                  pl.BlockSpec((1, wq), lambda b, i: (0, 0)),
                  pl.BlockSpec((1, GLA_DV), lambda b, i: (0, 0))],
        out_specs=pl.BlockSpec((c, wv), lambda b, i: (b * n + i, 0)),
        out_shape=jax.ShapeDtypeStruct((batch * seq, wv), BF16),
        scratch_shapes=[pltpu.VMEM((GLA_HEADS, GLA_DV, GLA_DK), F32)],
        compiler_params=_cparams(("parallel", "arbitrary")),
        name="gla",
    )(p, p, p, p, p, gate_up, gate_bias.reshape(1, wq), head_gain.reshape(1, GLA_DV))


def _bias_kernel(bucket_ref, rel_ref, o_ref):
    for h in range(DSA_HEADS):
        for off in range(3):
            bk = bucket_ref[off]

            def body(n, acc, bk=bk, h=h):
                return jnp.where(bk == n, rel_ref[n, h], acc)

            o_ref[h, off] = lax.fori_loop(0, N_BUCKETS, body, jnp.zeros(bk.shape, F32))


def _rel_bucket(dist):
    max_exact = N_BUCKETS // 2
    d = jnp.maximum(dist, 1).astype(F32)
    large = max_exact + (jnp.log(d / max_exact) / math.log(MAX_DISTANCE / max_exact)
                         * (N_BUCKETS - max_exact)).astype(jnp.int32)
    large = jnp.minimum(large, N_BUCKETS - 1)
    return jnp.where(dist < max_exact, dist, large)


def bias_tiles(rel_bias):
    s = jnp.arange(Q_BLOCK, dtype=jnp.int32)[:, None]
    t = jnp.arange(Q_BLOCK, dtype=jnp.int32)[None, :]
    dist = jnp.stack([off * Q_BLOCK + t - s for off in range(3)])
    bucket = _rel_bucket(jnp.maximum(dist, 0))
    return pl.pallas_call(
        _bias_kernel,
        in_specs=[pl.BlockSpec(memory_space=pltpu.VMEM), pl.BlockSpec(memory_space=pltpu.SMEM)],
        out_specs=pl.BlockSpec(memory_space=pltpu.VMEM),
        out_shape=jax.ShapeDtypeStruct((DSA_HEADS, 3, Q_BLOCK, Q_BLOCK), F32),
        name="bias_tiles",
    )(bucket, rel_bias)


def _dsa_kernel(iq_ref, smq_ref, sma_ref, dq_ref, dk_ref, dv_ref, bias_ref, o_ref,
                it_ref, key_ref, neg_ref, *, seq, topk):
    qb = Q_BLOCK
    i = pl.program_id(1)
    nkb = seq // qb
    kchunk = 512 if seq % 512 == 0 else qb
    s_loc = lax.broadcasted_iota(jnp.int32, (qb, qb), 0)
    t_loc = lax.broadcasted_iota(jnp.int32, (qb, qb), 1)
    lane_k = lax.broadcasted_iota(jnp.int32, (1, LANE), 1)

    sm_all = sma_ref[...]
    ik_even = jnp.where((lane_k >= 0) & (lane_k < IDX_DIM), pltpu.roll(sm_all, LANE - IK_LO, 1), 0.0).astype(BF16)
    ik_odd = jnp.where(lane_k >= IDX_DIM, pltpu.roll(sm_all, IDX_DIM - IK_LO, 1), 0.0).astype(BF16)
    w_t = smq_ref[...].T * ((IDX_HEADS ** -0.5) * (IDX_DIM ** -0.5))
    iq = iq_ref[...].astype(BF16)
    for kc in range(seq // kchunk):
        ka = ik_even[kc * kchunk:(kc + 1) * kchunk]
        kb = ik_odd[kc * kchunk:(kc + 1) * kchunk]
        acc = jnp.zeros((kchunk, qb), F32)
        for pr in range(IDX_HEADS // 2):
            qpair = iq[:, pr * LANE:(pr + 1) * LANE]
            he, ho = IW_LO + 2 * pr, IW_LO + 2 * pr + 1
            acc = acc + w_t[he:he + 1] * jnp.maximum(_dot_nt(ka, qpair), 0.0)
            acc = acc + w_t[ho:ho + 1] * jnp.maximum(_dot_nt(kb, qpair), 0.0)
        it_ref[kc * kchunk:(kc + 1) * kchunk, :] = acc

    for j in range(nkb):
        blk = it_ref[j * qb:(j + 1) * qb, :] + 0.0
        visible = (j * qb + s_loc) <= (i * qb + t_loc)
        blk = jnp.where(visible, blk, -jnp.inf)
        bits = pltpu.bitcast(blk, jnp.int32)
        key_ref[j * qb:(j + 1) * qb, :] = jnp.where(bits < 0, bits ^ jnp.int32(0x7FFFFFFF), bits)

    kf = float(topk)

    def count_ge(cand):
        return jnp.sum(jnp.where(key_ref[...] >= cand, 1.0, 0.0), axis=0, keepdims=True)

    int_min = jnp.int32(-2 ** 31)
    m0 = jnp.where(count_ge(jnp.zeros((1, qb), jnp.int32)) >= kf, jnp.int32(0), int_min)

    def search(n, m):
        cand = m | lax.shift_left(jnp.int32(1), jnp.int32(30) - n)
        return jnp.where(count_ge(cand) >= kf, cand, m)

    kth = lax.fori_loop(0, 31, search, m0)
    keys = key_ref[...]
    n_gt = jnp.sum(jnp.where(keys > kth, 1.0, 0.0), axis=0, keepdims=True)
    need = kf - n_gt

    tri = (t_loc <= s_loc).astype(BF16)
    seen = jnp.zeros((1, qb), F32)
    for j in range(nkb):
        kblk = key_ref[j * qb:(j + 1) * qb, :]
        eq = kblk == kth
        rank = jnp.dot(tri, jnp.where(eq, 1.0, 0.0).astype(BF16), preferred_element_type=F32) + seen
        seen = seen + jnp.sum(jnp.where(eq, 1.0, 0.0), axis=0, keepdims=True)
        visible = (j * qb + s_loc) <= (i * qb + t_loc)
        sel = ((kblk > kth) | (eq & (rank <= need))) & visible
        neg_ref[j * qb:(j + 1) * qb, :] = jnp.where(sel, 0.0, -jnp.inf)

    k_all = dk_ref[...].astype(BF16)
    v_all = dv_ref[...].astype(BF16)
    scale = DSA_DH ** -0.5
    for h in range(DSA_HEADS):
        qh = dq_ref[:, h * DSA_DH:(h + 1) * DSA_DH].astype(BF16)
        logit = _dot_nt(k_all, qh) * scale
        parts = []
        for j in range(nkb):
            off = jnp.clip(i - j, 0, 2)
            parts.append(logit[j * qb:(j + 1) * qb] + bias_ref[h, off] + neg_ref[j * qb:(j + 1) * qb, :])
        logit = jnp.concatenate(parts, axis=0)
        mx = jnp.max(logit, axis=0, keepdims=True)
        pexp = jnp.exp(logit - mx)
        den = jnp.sum(pexp, axis=0, keepdims=True)
        o_t = _dot_tn(v_all, pexp.astype(BF16)) / den
        o_ref[:, h * DSA_DH:(h + 1) * DSA_DH] = o_t.T.astype(o_ref.dtype)


def dsa(p, btiles, batch, seq):
    qb = Q_BLOCK
    nq = seq // qb
    topk = min(TOPK_MAX, seq // 4)
    wq = DSA_HEADS * DSA_DH
    wi = IDX_HEADS * IDX_DIM
    return pl.pallas_call(
        functools.partial(_dsa_kernel, seq=seq, topk=topk),
        grid=(batch, nq),
        in_specs=[pl.BlockSpec((qb, wi), lambda b, i: (b * nq + i, _DST["iq"] // wi)),
                  pl.BlockSpec((qb, LANE), lambda b, i: (b * nq + i, SMALL_OFF // LANE)),
                  pl.BlockSpec((seq, LANE), lambda b, i: (b, SMALL_OFF // LANE)),
                  pl.BlockSpec((qb, wq), lambda b, i: (b * nq + i, _DST["dq"] // wq)),
                  pl.BlockSpec((seq, DSA_DH), lambda b, i: (b, _DST["dk"] // DSA_DH)),
                  pl.BlockSpec((seq, DSA_DH), lambda b, i: (b, _DST["dv"] // DSA_DH)),
                  pl.BlockSpec((DSA_HEADS, 3, qb, qb), lambda b, i: (0, 0, 0, 0))],
        out_specs=pl.BlockSpec((qb, wq), lambda b, i: (b * nq + i, 0)),
        out_shape=jax.ShapeDtypeStruct((batch * seq, wq), BF16),
        scratch_shapes=[pltpu.VMEM((seq, qb), F32), pltpu.VMEM((seq, qb), jnp.int32),
                        pltpu.VMEM((seq, qb), F32)],
        compiler_params=_cparams(("parallel", "arbitrary")),
        name="dsa",
    )(p, p, p, p, p, p, btiles)


def _merge_kernel(oa_ref, ob_ref, oc_ref, ga_ref, gb_ref, gc_ref, wa_ref, wb_ref, wc_ref, o_ref):
    acc = None
    for o_r, g_r, w_r in ((oa_ref, ga_ref, wa_ref), (ob_ref, gb_ref, wb_ref), (oc_ref, gc_ref, wc_ref)):
        gate = 1.0 / (1.0 + jnp.exp(-g_r[...]))
        term = gate * jnp.dot(o_r[...], w_r[...], preferred_element_type=F32)
        acc = term if acc is None else acc + term
    o_ref[...] = acc.astype(o_ref.dtype)


def gated_merge(o_a, o_b, o_c, p, w_branch, tm=512, tn=1024):
    m = o_a.shape[0]
    nj = D_MODEL // tn
    o_spec = pl.BlockSpec((tm, BRANCH_WIDTH), lambda i, j: (i, 0))

    def g_spec(br):
        return pl.BlockSpec((tm, tn), lambda i, j: (i, _DST["gates"] // tn + br * nj + j))

    def w_spec(br):
        return pl.BlockSpec((None, BRANCH_WIDTH, tn), lambda i, j: (br, 0, j))

    return pl.pallas_call(
        _merge_kernel,
        grid=(m // tm, nj),
        in_specs=[o_spec, o_spec, o_spec, g_spec(0), g_spec(1), g_spec(2), w_spec(0), w_spec(1), w_spec(2)],
        out_specs=pl.BlockSpec((tm, tn), lambda i, j: (i, j)),
        out_shape=jax.ShapeDtypeStruct((m, D_MODEL), BF16),
        compiler_params=_cparams(("parallel", "parallel")),
        name="gated_merge",
    )(o_a, o_b, o_c, p, p, p, w_branch, w_branch, w_branch)


def _reorder_w_in(w):
    parts = [w[:, _SRC[n][0]:_SRC[n][0] + _SRC[n][1]] for n in _ORDER]
    used = sum(_SRC[n][1] for n in ("ga", "ik", "iw"))
    parts.append(jnp.zeros((w.shape[0], LANE - used), w.dtype))
    return jnp.concatenate(parts, axis=1).astype(BF16)


def mixer(h, x, l, btiles, norm_mix_post, norm_mlp_pre, w_in, gla_gate_up, gla_gate_bias, gla_head_gain,
          w_branch, w_out, batch, seq):
    p = matmul(h, _reorder_w_in(w_in[l]), F32, tm=1024, tn=1152, name="in_proj")
    o_a = gla(p, gla_gate_up[l], gla_gate_bias[l], gla_head_gain[l], batch, seq)
    o_b = dsa(p, btiles, batch, seq)
    o_c = stickbreaking(p, batch, seq)
    merged = gated_merge(o_a, o_b, o_c, p, w_branch[l].astype(BF16))
    y = matmul(merged, w_out[l].astype(BF16), F32, tm=1024, tn=1024, name="out_proj")
    return post_norm_residual(y, x, norm_mix_post[l], norm_mlp_pre[l])


def kernel(x, rel_bias, norm_mix_pre, norm_mix_post, norm_mlp_pre, norm_mlp_post, w_in, gla_gate_up,
           gla_gate_bias, gla_head_gain, w_branch, w_out, w_mlp_up, w_mlp_down):
    batch, seq, d = x.shape
    xf = x.reshape(batch * seq, d)
    btiles = bias_tiles(rel_bias)
    h = rmsnorm_cast(xf, norm_mix_pre[0])
    for l in range(DEPTH):
        xf, h = mixer(h, xf, l, btiles, norm_mix_post, norm_mlp_pre, w_in, gla_gate_up, gla_gate_bias,
                      gla_head_gain, w_branch, w_out, batch, seq)
        u = matmul(h, w_mlp_up[l].astype(BF16), BF16, tm=1024, tn=1024, sq_relu=True, name="mlp_up")
        y = matmul(u, w_mlp_down[l].astype(BF16), F32, tm=1024, tn=1024, tk=4096, name="mlp_down")
        g_next = norm_mix_pre[l + 1] if l + 1 < DEPTH else None
        xf, h = post_norm_residual(y, xf, norm_mlp_post[l], g_next)
    return xf.reshape(batch, seq, d)
```

```python
import functools
import math

import jax
import jax.numpy as jnp
from jax import lax
from jax.experimental import pallas as pl
from jax.experimental.pallas import tpu as pltpu

F32 = jnp.float32
BF16 = jnp.bfloat16

D_MODEL = 4096
DEPTH = 2
N_BRANCHES = 3
BRANCH_WIDTH = 1024
GLA_HEADS, GLA_DK, GLA_DV, GLA_RANK, GLA_TAU, GLA_CHUNK = 4, 128, 256, 16, 16.0, 64
GLA_SUB = 16
DSA_HEADS, DSA_DH, IDX_HEADS, IDX_DIM, TOPK_MAX = 8, 128, 32, 64, 256
SB_HEADS, SB_DH = 8, 128
Q_BLOCK = 128
N_BUCKETS, MAX_DISTANCE = 32, 128
D_FF = 4 * D_MODEL
EPS = 1e-6
LANE = 128

SRC_GLA = (0, 3072)
SRC_GA = 3072
SRC_DSA = (3088, 3328)
SRC_IK = 6416
SRC_SB = (6512, 3072)
SRC_GATES = (9584, 3 * D_MODEL)
IN_COLS = SRC_GATES[0] + SRC_GATES[1]
DSA_Q, DSA_K, DSA_V, DSA_IQ, DSA_W = 0, 1024, 1152, 1280, 3328
GA_LO, IK_LO, IW_LO = SRC_GA % LANE, SRC_IK % LANE, (SRC_IK + IDX_DIM) % LANE

VMEM_LIMIT = 56 * 1024 * 1024


def _cparams(sem):
    return pltpu.CompilerParams(dimension_semantics=sem, vmem_limit_bytes=VMEM_LIMIT)


def _cast_kernel(w_ref, o_ref):
    o_ref[...] = w_ref[...].astype(o_ref.dtype)


def cast_layer(w, l, tr=512, tc=4096):
    _, r, c = w.shape
    tc = min(tc, c)
    return pl.pallas_call(
        _cast_kernel,
        grid=(r // tr, c // tc),
        in_specs=[pl.BlockSpec((None, tr, tc), lambda i, j: (l, i, j))],
        out_specs=pl.BlockSpec((tr, tc), lambda i, j: (i, j)),
        out_shape=jax.ShapeDtypeStruct((r, c), BF16),
        compiler_params=_cparams(("parallel", "parallel")),
        name="cast_weight",
    )(w)


def _realign_kernel(main_ref, extra_ref, o_ref, *, shift, nblk):
    lane = lax.broadcasted_iota(jnp.int32, (1, LANE), 1)
    for n in range(nblk):
        a = main_ref[:, n * LANE:(n + 1) * LANE]
        b = extra_ref[...] if n == nblk - 1 else main_ref[:, (n + 1) * LANE:(n + 2) * LANE]
        if shift:
            a = jnp.where(lane < LANE - shift, pltpu.roll(a, LANE - shift, 1), pltpu.roll(b, LANE - shift, 1))
        o_ref[:, n * LANE:(n + 1) * LANE] = a.astype(o_ref.dtype)


def realign_cast(w_in, l, src, nblk=2):
    off, width = src
    a, shift = divmod(off, LANE)
    rows = w_in.shape[1]
    wb = nblk * LANE
    assert a % nblk == 0 and width % wb == 0
    return pl.pallas_call(
        functools.partial(_realign_kernel, shift=shift, nblk=nblk),
        grid=(width // wb,),
        in_specs=[pl.BlockSpec((None, rows, wb), lambda n: (l, 0, a // nblk + n)),
                  pl.BlockSpec((None, rows, LANE), lambda n: (l, 0, a + nblk * (n + 1)))],
        out_specs=pl.BlockSpec((rows, wb), lambda n: (0, n)),
        out_shape=jax.ShapeDtypeStruct((rows, width), BF16),
        compiler_params=_cparams(("parallel",)),
        name="realign_weight",
    )(w_in, w_in)


def _small_kernel(ga_ref, ik_ref, o_ref):
    lane = lax.broadcasted_iota(jnp.int32, (1, LANE), 1)
    keep_ik = (lane >= IK_LO) & (lane < IW_LO + IDX_HEADS)
    o_ref[...] = jnp.where(lane < GA_LO + GLA_RANK, ga_ref[...], jnp.where(keep_ik, ik_ref[...], 0.0)
                           ).astype(o_ref.dtype)


def small_group_weight(w_in, l):
    rows = w_in.shape[1]
    assert GA_LO + GLA_RANK <= IK_LO
    return pl.pallas_call(
        _small_kernel,
        grid=(1,),
        in_specs=[pl.BlockSpec((None, rows, LANE), lambda n: (l, 0, SRC_GA // LANE)),
                  pl.BlockSpec((None, rows, LANE), lambda n: (l, 0, SRC_IK // LANE))],
        out_specs=pl.BlockSpec((rows, LANE), lambda n: (0, 0)),
        out_shape=jax.ShapeDtypeStruct((rows, LANE), BF16),
        compiler_params=_cparams(("arbitrary",)),
        name="small_weight",
    )(w_in, w_in)


def _rms_kernel(x_ref, g_ref, o_ref):
    x = x_ref[...]
    y = x * lax.rsqrt(jnp.mean(x * x, axis=-1, keepdims=True) + EPS)
    o_ref[...] = (y * g_ref[...]).astype(o_ref.dtype)


def rmsnorm_cast(x, g, tm=256):
    m, d = x.shape
    return pl.pallas_call(
        _rms_kernel,
        grid=(m // tm,),
        in_specs=[pl.BlockSpec((tm, d), lambda i: (i, 0)), pl.BlockSpec((1, d), lambda i: (0, 0))],
        out_specs=pl.BlockSpec((tm, d), lambda i: (i, 0)),
        out_shape=jax.ShapeDtypeStruct((m, d), BF16),
        compiler_params=_cparams(("parallel",)),
        name="rmsnorm_cast",
    )(x, g.reshape(1, d))


def _post_kernel(y_ref, x_ref, gp_ref, gn_ref, xo_ref, ho_ref):
    y = y_ref[...]
    yn = y * lax.rsqrt(jnp.mean(y * y, axis=-1, keepdims=True) + EPS) * gp_ref[...]
    xn = x_ref[...] + yn
    xo_ref[...] = xn
    hn = xn * lax.rsqrt(jnp.mean(xn * xn, axis=-1, keepdims=True) + EPS) * gn_ref[...]
    ho_ref[...] = hn.astype(ho_ref.dtype)


def _post_last_kernel(y_ref, x_ref, gp_ref, xo_ref):
    y = y_ref[...]
    yn = y * lax.rsqrt(jnp.mean(y * y, axis=-1, keepdims=True) + EPS) * gp_ref[...]
    xo_ref[...] = x_ref[...] + yn


def post_norm_residual(y, x, g_post, g_next, tm=256):
    m, d = x.shape
    row = pl.BlockSpec((tm, d), lambda i: (i, 0))
    vec = pl.BlockSpec((1, d), lambda i: (0, 0))
    if g_next is None:
        return pl.pallas_call(
            _post_last_kernel, grid=(m // tm,), in_specs=[row, row, vec], out_specs=row,
            out_shape=jax.ShapeDtypeStruct((m, d), F32),
            compiler_params=_cparams(("parallel",)), name="post_last",
        )(y, x, g_post.reshape(1, d)), None
    return pl.pallas_call(
        _post_kernel, grid=(m // tm,), in_specs=[row, row, vec, vec], out_specs=(row, row),
        out_shape=(jax.ShapeDtypeStruct((m, d), F32), jax.ShapeDtypeStruct((m, d), BF16)),
        compiler_params=_cparams(("parallel",)), name="post_norm",
    )(y, x, g_post.reshape(1, d), g_next.reshape(1, d))


def _mm_kernel(x_ref, w_ref, o_ref, *, sq_relu):
    r = jnp.dot(x_ref[...], w_ref[...], preferred_element_type=F32)
    if sq_relu:
        r = jnp.square(jnp.maximum(r, 0.0))
    o_ref[...] = r.astype(o_ref.dtype)


def _mm_acc_kernel(x_ref, w_ref, o_ref, acc_ref, *, nk):
    k = pl.program_id(2)

    @pl.when(k == 0)
    def _():
        acc_ref[...] = jnp.zeros_like(acc_ref)

    acc_ref[...] += jnp.dot(x_ref[...], w_ref[...], preferred_element_type=F32)

    @pl.when(k == nk - 1)
    def _():
        o_ref[...] = acc_ref[...].astype(o_ref.dtype)


def matmul(x, w, out_dtype, tm, tn, tk=None, sq_relu=False, name="matmul"):
    m, kd = x.shape
    _, n = w.shape
    assert m % tm == 0 and n % tn == 0
    if tk is None or tk == kd:
        return pl.pallas_call(
            functools.partial(_mm_kernel, sq_relu=sq_relu),
            grid=(m // tm, n // tn),
            in_specs=[pl.BlockSpec((tm, kd), lambda i, j: (i, 0)),
                      pl.BlockSpec((kd, tn), lambda i, j: (0, j))],
            out_specs=pl.BlockSpec((tm, tn), lambda i, j: (i, j)),
            out_shape=jax.ShapeDtypeStruct((m, n), out_dtype),
            compiler_params=_cparams(("parallel", "parallel")),
            name=name,
        )(x, w)
    nk = kd // tk
    return pl.pallas_call(
        functools.partial(_mm_acc_kernel, nk=nk),
        grid=(m // tm, n // tn, nk),
        in_specs=[pl.BlockSpec((tm, tk), lambda i, j, k: (i, k)),
                  pl.BlockSpec((tk, tn), lambda i, j, k: (k, j))],
        out_specs=pl.BlockSpec((tm, tn), lambda i, j, k: (i, j)),
        out_shape=jax.ShapeDtypeStruct((m, n), out_dtype),
        scratch_shapes=[pltpu.VMEM((tm, tn), F32)],
        compiler_params=_cparams(("parallel", "parallel", "arbitrary")),
        name=name,
    )(x, w)


def _split_bf16(x):
    hi = x.astype(BF16)
    lo = (x - hi.astype(F32)).astype(BF16)
    return hi, lo


def _log_sigmoid_pair(z):
    sp = jnp.log(1.0 + jnp.exp(-jnp.abs(z)))
    return jnp.minimum(z, 0.0) - sp, -jnp.maximum(z, 0.0) - sp


def _dot_nt(a, b):
    return lax.dot_general(a, b, (((1,), (1,)), ((), ())), preferred_element_type=F32)


def _dot_tn(a, b):
    return lax.dot_general(a, b, (((0,), (0,)), ((), ())), preferred_element_type=F32)


def _sb_kernel(q_ref, k_ref, v_ref, o_ref, *, tq, nh):
    qi = pl.program_id(2)
    dh = SB_DH
    scale = dh ** -0.5
    nsub = tq // LANE
    rr = lax.broadcasted_iota(jnp.int32, (LANE, 2 * LANE), 0)
    cc = lax.broadcasted_iota(jnp.int32, (LANE, 2 * LANE), 1)
    cum_rhs = ((cc >= LANE) | (rr > cc)).astype(BF16)
    rows = lax.broadcasted_iota(jnp.int32, (tq, tq), 0)
    cols = lax.broadcasted_iota(jnp.int32, (tq, tq), 1)
    strict = cols < rows

    def tile(j, state, diag):
        start = pl.multiple_of(j * tq, tq)
        new_state = []
        for h in range(nh):
            run, acc = state[h]
            hs = slice(h * dh, (h + 1) * dh)
            z = _dot_nt(q_ref[:, hs], k_ref[pl.ds(start, tq), hs]) * scale
            log_beta, log_1m = _log_sigmoid_pair(z)
            if diag:
                log_1m = jnp.where(strict, log_1m, 0.0)
            hi, lo = _split_bf16(log_1m)
            after = [None] * nsub
            for sb in reversed(range(nsub)):
                cs = slice(sb * LANE, (sb + 1) * LANE)
                r2 = (jnp.dot(hi[:, cs], cum_rhs, preferred_element_type=F32)
                      + jnp.dot(lo[:, cs], cum_rhs, preferred_element_type=F32))
                after[sb] = r2[:, :LANE] + run
                run = run + r2[:, LANE:]
            w = jnp.exp(log_beta + jnp.concatenate(after, axis=1))
            if diag:
                w = jnp.where(strict, w, 0.0)
            acc = acc + jnp.dot(w.astype(BF16), v_ref[pl.ds(start, tq), hs], preferred_element_type=F32)
            new_state.append((run, acc))
        return tuple(new_state)

    zero = jnp.zeros((tq, LANE), F32)
    state = tile(qi, tuple((zero, zero) for _ in range(nh)), True)
    state = lax.fori_loop(1, qi + 1, lambda jj, s: tile(qi - jj, s, False), state)
    for h in range(nh):
        o_ref[:, h * dh:(h + 1) * dh] = state[h][1].astype(o_ref.dtype)


def stickbreaking(p_sb, batch, seq, tq=256, nh=2):
    assert SB_DH == LANE
    nq = seq // tq
    ng = SB_HEADS // nh
    wb = nh * SB_DH
    return pl.pallas_call(
        functools.partial(_sb_kernel, tq=tq, nh=nh),
        grid=(batch, ng, nq),
        in_specs=[pl.BlockSpec((tq, wb), lambda b, g, i: (b * nq + i, g)),
                  pl.BlockSpec((seq, wb), lambda b, g, i: (b, ng + g)),
                  pl.BlockSpec((seq, wb), lambda b, g, i: (b, 2 * ng + g))],
        out_specs=pl.BlockSpec((tq, wb), lambda b, g, i: (b * nq + i, g)),
        out_shape=jax.ShapeDtypeStruct((batch * seq, SB_HEADS * SB_DH), BF16),
        compiler_params=_cparams(("parallel", "parallel", "arbitrary")),
        name="stickbreaking",
    )(p_sb, p_sb, p_sb)


def _gla_kernel(q_ref, k_ref, v_ref, g_ref, small_ref, up_ref, bias_ref, gain_ref, o_ref, st_ref):
    c = GLA_CHUNK

    @pl.when(pl.program_id(1) == 0)
    def _():
        st_ref[...] = jnp.zeros_like(st_ref)

    a_hi, a_lo = _split_bf16(small_ref[:, GA_LO:GA_LO + GLA_RANK])
    u_hi, u_lo = _split_bf16(up_ref[...])
    x = (jnp.dot(a_hi, u_hi, preferred_element_type=F32) + jnp.dot(a_hi, u_lo, preferred_element_type=F32)
         + jnp.dot(a_lo, u_hi, preferred_element_type=F32)) + bias_ref[...]
    log_a = _log_sigmoid_pair(x)[0] / GLA_TAU
    r_i = lax.broadcasted_iota(jnp.int32, (c, c), 0)
    c_i = lax.broadcasted_iota(jnp.int32, (c, c), 1)
    tril = (c_i <= r_i).astype(BF16)
    g_hi, g_lo = _split_bf16(log_a)
    b_all = jnp.dot(tril, g_hi, preferred_element_type=F32) + jnp.dot(tril, g_lo, preferred_element_type=F32)

    sub = GLA_SUB
    lane = lax.broadcasted_iota(jnp.int32, (sub, LANE), 1)
    jrow = lax.broadcasted_iota(jnp.int32, (sub, LANE), 0)
    gain = gain_ref[...]

    for h in range(GLA_HEADS):
        b = b_all[:, h * GLA_DK:(h + 1) * GLA_DK]
        q = q_ref[:, h * GLA_DK:(h + 1) * GLA_DK] * (GLA_DK ** -0.5)
        k = k_ref[:, h * GLA_DK:(h + 1) * GLA_DK]
        v = v_ref[:, h * GLA_DV:(h + 1) * GLA_DV].astype(BF16)
        st = st_ref[h]
        o_inter = _dot_nt((q * jnp.exp(b)).astype(BF16), st.astype(BF16))

        o_rows = []
        for blk in range(c // sub):
            lo_r, hi_r = blk * sub, (blk + 1) * sub
            b_blk, q_blk, k_blk = b[lo_r:hi_r], q[lo_r:hi_r], k[lo_r:hi_r]
            o_blk = o_inter[lo_r:hi_r]
            if blk > 0:
                b_first = b[lo_r:lo_r + 1]
                qt = (q_blk * jnp.exp(b_blk - b_first)).astype(BF16)
                kt = (k[:lo_r] * jnp.exp(b_first - b[:lo_r])).astype(BF16)
                a_off = _dot_nt(qt, kt)
                o_blk = o_blk + jnp.dot(a_off.astype(BF16), v[:lo_r], preferred_element_type=F32)
            a_t = jnp.zeros((sub, LANE), F32)
            for i in range(sub):
                e = jnp.exp(jnp.minimum(b_blk[i:i + 1] - b_blk, 0.0))
                col = jnp.sum(q_blk[i:i + 1] * k_blk * e, axis=-1, keepdims=True)
                a_t = jnp.where(lane == i, col, a_t)
            a_t = jnp.where(jrow <= lane, a_t, 0.0)
            o_diag = _dot_tn(a_t.astype(BF16), v[lo_r:hi_r])
            o_rows.append(o_blk + o_diag[:sub])
        o = jnp.concatenate(o_rows, axis=0)

        b_last = b[c - 1:c]
        kd = (k * jnp.exp(b_last - b)).astype(BF16)
        st_ref[h] = st * jnp.exp(b_last) + _dot_tn(v, kd)

        o = o * lax.rsqrt(jnp.mean(o * o, axis=-1, keepdims=True) + EPS) * gain
        g = g_ref[:, h * GLA_DV:(h + 1) * GLA_DV]
        o = o * (g / (1.0 + jnp.exp(-g)))
        o_ref[:, h * GLA_DV:(h + 1) * GLA_DV] = o.astype(o_ref.dtype)


def gla(p_gla, p_small, gate_up, gate_bias, head_gain, batch, seq):
    c = GLA_CHUNK
    n = seq // c
    wq, wv = GLA_HEADS * GLA_DK, GLA_HEADS * GLA_DV
    return pl.pallas_call(
        _gla_kernel,
        grid=(batch, n),
        in_specs=[pl.BlockSpec((c, wq), lambda b, i: (b * n + i, 0)),
                  pl.BlockSpec((c, wq), lambda b, i: (b * n + i, 1)),
                  pl.BlockSpec((c, wv), lambda b, i: (b * n + i, 1)),
                  pl.BlockSpec((c, wv), lambda b, i: (b * n + i, 2)),
                  pl.BlockSpec((c, LANE), lambda b, i: (b * n + i, 0)),
                  pl.BlockSpec((GLA_RANK, wq), lambda b, i: (0, 0)),
                  pl.BlockSpec((1, wq), lambda b, i: (0, 0)),
                  pl.BlockSpec((1, GLA_DV), lambda b, i: (0, 0))],
        out_specs=pl.BlockSpec((c, wv), lambda b, i: (b * n + i, 0)),
        out_shape=jax.ShapeDtypeStruct((batch * seq, wv), BF16),
        scratch_shapes=[pltpu.VMEM((GLA_HEADS, GLA_DV, GLA_DK), F32)],
        compiler_params=_cparams(("parallel", "arbitrary")),
        name="gla",
    )(p_gla, p_gla, p_gla, p_gla, p_small, gate_up, gate_bias.reshape(1, wq), head_gain.reshape(1, GLA_DV))


def _bias_kernel(bucket_ref, rel_ref, o_ref):
    for h in range(DSA_HEADS):
        for off in range(3):
            bk = bucket_ref[off]

            def body(n, acc, bk=bk, h=h):
                return jnp.where(bk == n, rel_ref[n, h], acc)

            o_ref[h, off] = lax.fori_loop(0, N_BUCKETS, body, jnp.zeros(bk.shape, F32))


def _rel_bucket(dist):
    max_exact = N_BUCKETS // 2
    d = jnp.maximum(dist, 1).astype(F32)
    large = max_exact + (jnp.log(d / max_exact) / math.log(MAX_DISTANCE / max_exact)
                         * (N_BUCKETS - max_exact)).astype(jnp.int32)
    large = jnp.minimum(large, N_BUCKETS - 1)
    return jnp.where(dist < max_exact, dist, large)


def bias_tiles(rel_bias):
    s = jnp.arange(Q_BLOCK, dtype=jnp.int32)[:, None]
    t = jnp.arange(Q_BLOCK, dtype=jnp.int32)[None, :]
    dist = jnp.stack([off * Q_BLOCK + t - s for off in range(3)])
    bucket = _rel_bucket(jnp.maximum(dist, 0))
    return pl.pallas_call(
        _bias_kernel,
        in_specs=[pl.BlockSpec(memory_space=pltpu.VMEM), pl.BlockSpec(memory_space=pltpu.SMEM)],
        out_specs=pl.BlockSpec(memory_space=pltpu.VMEM),
        out_shape=jax.ShapeDtypeStruct((DSA_HEADS, 3, Q_BLOCK, Q_BLOCK), F32),
        name="bias_tiles",
    )(bucket, rel_bias)


DSA_CHUNK = 2 * Q_BLOCK
NEG_BIG = -1e30


def _dsa_kernel(row_ref, smq_ref, sma_ref, dk_ref, dv_ref, bias_ref, o_ref,
                key_ref, neg_ref, q2_ref, acc_ref, *, topk):
    qb, ck = Q_BLOCK, DSA_CHUNK
    i = pl.program_id(1)
    nck = (i + 2) // 2
    s_loc = lax.broadcasted_iota(jnp.int32, (ck, qb), 0)
    t_loc = lax.broadcasted_iota(jnp.int32, (ck, qb), 1)
    lane_k = lax.broadcasted_iota(jnp.int32, (1, LANE), 1)

    def visible(r0):
        return (r0 + s_loc) <= (i * qb + t_loc)

    w_t = smq_ref[...].T * ((IDX_HEADS ** -0.5) * (IDX_DIM ** -0.5))
    for pr in range(IDX_HEADS // 2):
        qpair = row_ref[:, DSA_IQ + pr * LANE:DSA_IQ + (pr + 1) * LANE]
        zero = jnp.zeros_like(qpair)
        q2_ref[pr, :qb, :] = jnp.where(lane_k < IDX_DIM, qpair, zero)
        q2_ref[pr, qb:, :] = jnp.where(lane_k >= IDX_DIM, qpair, zero)

    def score_chunk(c, carry):
        r0 = pl.multiple_of(c * ck, ck)
        sm = sma_ref[pl.ds(r0, ck), :]
        k_dup = jnp.where(lane_k < IDX_DIM, pltpu.roll(sm, LANE - IK_LO, 1),
                          pltpu.roll(sm, IDX_DIM - IK_LO, 1)).astype(BF16)
        acc = jnp.zeros((ck, qb), F32)
        for pr in range(IDX_HEADS // 2):
            s2 = jnp.maximum(_dot_nt(k_dup, q2_ref[pr]), 0.0)
            he, ho = IW_LO + 2 * pr, IW_LO + 2 * pr + 1
            acc = acc + w_t[he:he + 1] * s2[:, :qb] + w_t[ho:ho + 1] * s2[:, qb:]
        val = jnp.where(visible(r0), acc + 0.0, -jnp.inf)
        bits = pltpu.bitcast(val, jnp.int32)
        key_ref[pl.ds(r0, ck), :] = jnp.where(bits < 0, bits ^ jnp.int32(0x7FFFFFFF), bits)
        return carry

    lax.fori_loop(0, nck, score_chunk, 0)

    kf = float(topk)

    def count(pred):
        def body(c, acc):
            r0 = pl.multiple_of(c * ck, ck)
            return acc + jnp.where(pred(key_ref[pl.ds(r0, ck), :]), 1.0, 0.0)
        acc = lax.fori_loop(0, nck, body, jnp.zeros((ck, qb), F32))
        return jnp.sum(acc, axis=0, keepdims=True)

    int_min = jnp.int32(-2 ** 31)
    m0 = jnp.where(count(lambda kk: kk >= 0) >= kf, jnp.int32(0), int_min)

    def search(n, m):
        cand = m | lax.shift_left(jnp.int32(1), jnp.int32(30) - n)
        return jnp.where(count(lambda kk: kk >= cand) >= kf, cand, m)

    kth = lax.fori_loop(0, 31, search, m0)
    need = kf - count(lambda kk: kk > kth)

    tri_r = lax.broadcasted_iota(jnp.int32, (ck, ck), 0)
    tri_c = lax.broadcasted_iota(jnp.int32, (ck, ck), 1)
    tri = (tri_c <= tri_r).astype(BF16)

    def select_chunk(c, seen):
        r0 = pl.multiple_of(c * ck, ck)
        kblk = key_ref[pl.ds(r0, ck), :]
        eq = jnp.where(kblk == kth, 1.0, 0.0)
        rank = jnp.dot(tri, eq.astype(BF16), preferred_element_type=F32) + seen
        keep = jnp.where(kblk > kth, 1.0, jnp.where(rank <= need, eq, 0.0))
        keep = jnp.where(visible(r0), keep, 0.0)
        neg_ref[pl.ds(r0, ck), :] = jnp.where(keep > 0.0, 0.0, -jnp.inf)
        return seen + jnp.sum(eq, axis=0, keepdims=True)

    lax.fori_loop(0, nck, select_chunk, jnp.zeros((1, qb), F32))

    acc_ref[...] = jnp.zeros_like(acc_ref)
    scale = DSA_DH ** -0.5

    def attend_chunk(c, ml):
        r0 = pl.multiple_of(c * ck, ck)
        kc = dk_ref[pl.ds(r0, ck), :]
        vc = dv_ref[pl.ds(r0, ck), :]
        neg = neg_ref[pl.ds(r0, ck), :]
        off0 = jnp.clip(i - 2 * c, 0, 2)
        off1 = jnp.clip(i - 2 * c - 1, 0, 2)
        new_ml = []
        for h in range(DSA_HEADS):
            m_old, l_old = ml[h]
            bias = jnp.concatenate([bias_ref[h, off0], bias_ref[h, off1]], axis=0)
            qh = row_ref[:, DSA_Q + h * DSA_DH:DSA_Q + (h + 1) * DSA_DH]
            lg = _dot_nt(kc, qh) * scale + bias + neg
            m_new = jnp.maximum(m_old, jnp.max(lg, axis=0, keepdims=True))
            alpha = jnp.exp(m_old - m_new)
            pexp = jnp.exp(lg - m_new)
            acc_ref[h] = alpha * acc_ref[h] + _dot_tn(vc, pexp.astype(BF16))
            new_ml.append((m_new, alpha * l_old + jnp.sum(pexp, axis=0, keepdims=True)))
        return tuple(new_ml)

    ml0 = tuple((jnp.full((1, qb), NEG_BIG, F32), jnp.zeros((1, qb), F32)) for _ in range(DSA_HEADS))
    ml = lax.fori_loop(0, nck, attend_chunk, ml0)
    for h in range(DSA_HEADS):
        o_t = acc_ref[h] / ml[h][1]
        o_ref[:, h * DSA_DH:(h + 1) * DSA_DH] = o_t.T.astype(o_ref.dtype)


def dsa(p_dsa, p_small, btiles, batch, seq):
    qb = Q_BLOCK
    nq = seq // qb
    assert seq % DSA_CHUNK == 0 and DSA_DH == LANE
    topk = min(TOPK_MAX, seq // 4)
    wq = DSA_HEADS * DSA_DH
    return pl.pallas_call(
        functools.partial(_dsa_kernel, topk=topk),
        grid=(batch, nq),
        in_specs=[pl.BlockSpec((qb, DSA_W), lambda b, i: (b * nq + i, 0)),
                  pl.BlockSpec((qb, LANE), lambda b, i: (b * nq + i, 0)),
                  pl.BlockSpec((seq, LANE), lambda b, i: (b, 0)),
                  pl.BlockSpec((seq, DSA_DH), lambda b, i: (b, DSA_K // DSA_DH)),
                  pl.BlockSpec((seq, DSA_DH), lambda b, i: (b, DSA_V // DSA_DH)),
                  pl.BlockSpec((DSA_HEADS, 3, qb, qb), lambda b, i: (0, 0, 0, 0))],
        out_specs=pl.BlockSpec((qb, wq), lambda b, i: (b * nq + i, 0)),
        out_shape=jax.ShapeDtypeStruct((batch * seq, wq), BF16),
        scratch_shapes=[pltpu.VMEM((seq, qb), jnp.int32), pltpu.VMEM((seq, qb), F32),
                        pltpu.VMEM((IDX_HEADS // 2, 2 * qb, LANE), BF16),
                        pltpu.VMEM((DSA_HEADS, DSA_DH, qb), F32)],
        compiler_params=_cparams(("parallel", "arbitrary")),
        name="dsa",
    )(p_dsa, p_small, p_small, p_dsa, p_dsa, btiles)


def _merge_kernel(oa_ref, ob_ref, oc_ref, ga_ref, gb_ref, gc_ref, wa_ref, wb_ref, wc_ref, o_ref):
    acc = None
    for o_r, g_r, w_r in ((oa_ref, ga_ref, wa_ref), (ob_ref, gb_ref, wb_ref), (oc_ref, gc_ref, wc_ref)):
        gate = 1.0 / (1.0 + jnp.exp(-g_r[...]))
        term = gate * jnp.dot(o_r[...], w_r[...], preferred_element_type=F32)
        acc = term if acc is None else acc + term
    o_ref[...] = acc.astype(o_ref.dtype)


def gated_merge(o_a, o_b, o_c, p_gates, w_branch, tm=512, tn=1024):
    m = o_a.shape[0]
    nj = D_MODEL // tn
    o_spec = pl.BlockSpec((tm, BRANCH_WIDTH), lambda i, j: (i, 0))

    def g_spec(br):
        return pl.BlockSpec((tm, tn), lambda i, j: (i, br * nj + j))

    def w_spec(br):
        return pl.BlockSpec((BRANCH_WIDTH, tn), lambda i, j: (br, j))

    return pl.pallas_call(
        _merge_kernel,
        grid=(m // tm, nj),
        in_specs=[o_spec, o_spec, o_spec, g_spec(0), g_spec(1), g_spec(2), w_spec(0), w_spec(1), w_spec(2)],
        out_specs=pl.BlockSpec((tm, tn), lambda i, j: (i, j)),
        out_shape=jax.ShapeDtypeStruct((m, D_MODEL), BF16),
        compiler_params=_cparams(("parallel", "parallel")),
        name="gated_merge",
    )(o_a, o_b, o_c, p_gates, p_gates, p_gates, w_branch, w_branch, w_branch)


def mixer(h, x, l, btiles, norm_mix_post, norm_mlp_pre, w_in, gla_gate_up, gla_gate_bias, gla_head_gain,
          w_branch, w_out, batch, seq):
    p_gla = matmul(h, realign_cast(w_in, l, SRC_GLA), F32, tm=1024, tn=1024, name="proj_gla")
    p_dsa = matmul(h, realign_cast(w_in, l, SRC_DSA), BF16, tm=512, tn=DSA_W // 2, name="proj_dsa")
    p_sb = matmul(h, realign_cast(w_in, l, SRC_SB), BF16, tm=1024, tn=1024, name="proj_sb")
    p_gates = matmul(h, realign_cast(w_in, l, SRC_GATES), F32, tm=1024, tn=1024, name="proj_gates")
    p_small = matmul(h, small_group_weight(w_in, l), F32, tm=1024, tn=LANE, name="proj_small")
    o_a = gla(p_gla, p_small, gla_gate_up[l], gla_gate_bias[l], gla_head_gain[l], batch, seq)
    o_b = dsa(p_dsa, p_small, btiles, batch, seq)
    o_c = stickbreaking(p_sb, batch, seq)
    wb = cast_layer(w_branch.reshape(DEPTH, N_BRANCHES * BRANCH_WIDTH, D_MODEL), l)
    merged = gated_merge(o_a, o_b, o_c, p_gates, wb)
    y = matmul(merged, cast_layer(w_out, l), F32, tm=1024, tn=1024, name="out_proj")
    return post_norm_residual(y, x, norm_mix_post[l], norm_mlp_pre[l])


def kernel(x, rel_bias, norm_mix_pre, norm_mix_post, norm_mlp_pre, norm_mlp_post, w_in, gla_gate_up,
           gla_gate_bias, gla_head_gain, w_branch, w_out, w_mlp_up, w_mlp_down):
    batch, seq, d = x.shape
    xf = x.reshape(batch * seq, d)
    btiles = bias_tiles(rel_bias)
    h = rmsnorm_cast(xf, norm_mix_pre[0])
    for l in range(DEPTH):
        xf, h = mixer(h, xf, l, btiles, norm_mix_post, norm_mlp_pre, w_in, gla_gate_up, gla_gate_bias,
                      gla_head_gain, w_branch, w_out, batch, seq)
        u = matmul(h, cast_layer(w_mlp_up, l), BF16, tm=1024, tn=1024, sq_relu=True, name="mlp_up")
        y = matmul(u, cast_layer(w_mlp_down, l), F32, tm=1024, tn=1024, tk=4096, name="mlp_down")
        g_next = norm_mix_pre[l + 1] if l + 1 < DEPTH else None
        xf, h = post_norm_residual(y, xf, norm_mlp_post[l], g_next)
    return xf.reshape(batch, seq, d)
```

```python
import functools
import math

import jax
import jax.numpy as jnp
from jax import lax
from jax.experimental import pallas as pl
from jax.experimental.pallas import tpu as pltpu

F32 = jnp.float32
BF16 = jnp.bfloat16

D_MODEL = 4096
DEPTH = 2
N_BRANCHES = 3
BRANCH_WIDTH = 1024
GLA_HEADS, GLA_DK, GLA_DV, GLA_RANK, GLA_TAU, GLA_CHUNK = 4, 128, 256, 16, 16.0, 64
GLA_SUB = 16
DSA_HEADS, DSA_DH, IDX_HEADS, IDX_DIM, TOPK_MAX = 8, 128, 32, 64, 256
SB_HEADS, SB_DH = 8, 128
Q_BLOCK = 128
N_BUCKETS, MAX_DISTANCE = 32, 128
D_FF = 4 * D_MODEL
EPS = 1e-6
LANE = 128

PROJ_TN = 512
SRC_GLA = (0, 3072)
SRC_GA = 3072
SRC_DSA = (3088, 3584)
SRC_IK = 6416
SRC_SB = (6512, 3072)
SRC_GATES = (9584, 3 * D_MODEL)
IN_COLS = SRC_GATES[0] + SRC_GATES[1]
DSA_Q, DSA_K, DSA_V, DSA_IQ, DSA_W = 0, 1024, 1152, 1280, 3328
SMALL_ROWS = (SRC_GA, SRC_IK - SRC_IK % LANE)
GA_LO, IK_LO, IW_LO = 0, SRC_IK % LANE, (SRC_IK + IDX_DIM) % LANE

VMEM_LIMIT = 56 * 1024 * 1024


def _cparams(sem):
    return pltpu.CompilerParams(dimension_semantics=sem, vmem_limit_bytes=VMEM_LIMIT)


def _cast_kernel(w_ref, o_ref):
    o_ref[...] = w_ref[...].astype(o_ref.dtype)


def cast_layer(w, l, tr=512, tc=4096):
    _, r, c = w.shape
    tc = min(tc, c)
    return pl.pallas_call(
        _cast_kernel,
        grid=(r // tr, c // tc),
        in_specs=[pl.BlockSpec((None, tr, tc), lambda i, j: (l, i, j))],
        out_specs=pl.BlockSpec((tr, tc), lambda i, j: (i, j)),
        out_shape=jax.ShapeDtypeStruct((r, c), BF16),
        compiler_params=_cparams(("parallel", "parallel")),
        name="cast_weight",
    )(w)


def _rms_kernel(x_ref, g_ref, o_ref):
    x = x_ref[...]
    y = x * lax.rsqrt(jnp.mean(x * x, axis=-1, keepdims=True) + EPS)
    o_ref[...] = (y * g_ref[...]).astype(o_ref.dtype)


def rmsnorm_cast(x, g, tm=256):
    m, d = x.shape
    return pl.pallas_call(
        _rms_kernel,
        grid=(m // tm,),
        in_specs=[pl.BlockSpec((tm, d), lambda i: (i, 0)), pl.BlockSpec((1, d), lambda i: (0, 0))],
        out_specs=pl.BlockSpec((tm, d), lambda i: (i, 0)),
        out_shape=jax.ShapeDtypeStruct((m, d), BF16),
        compiler_params=_cparams(("parallel",)),
        name="rmsnorm_cast",
    )(x, g.reshape(1, d))


def _post_kernel(y_ref, x_ref, gp_ref, gn_ref, xo_ref, ho_ref):
    y = y_ref[...]
    yn = y * lax.rsqrt(jnp.mean(y * y, axis=-1, keepdims=True) + EPS) * gp_ref[...]
    xn = x_ref[...] + yn
    xo_ref[...] = xn
    hn = xn * lax.rsqrt(jnp.mean(xn * xn, axis=-1, keepdims=True) + EPS) * gn_ref[...]
    ho_ref[...] = hn.astype(ho_ref.dtype)


def _post_last_kernel(y_ref, x_ref, gp_ref, xo_ref):
    y = y_ref[...]
    yn = y * lax.rsqrt(jnp.mean(y * y, axis=-1, keepdims=True) + EPS) * gp_ref[...]
    xo_ref[...] = x_ref[...] + yn


def post_norm_residual(y, x, g_post, g_next, tm=256):
    m, d = x.shape
    row = pl.BlockSpec((tm, d), lambda i: (i, 0))
    vec = pl.BlockSpec((1, d), lambda i: (0, 0))
    if g_next is None:
        return pl.pallas_call(
            _post_last_kernel, grid=(m // tm,), in_specs=[row, row, vec], out_specs=row,
            out_shape=jax.ShapeDtypeStruct((m, d), F32),
            compiler_params=_cparams(("parallel",)), name="post_last",
        )(y, x, g_post.reshape(1, d)), None
    return pl.pallas_call(
        _post_kernel, grid=(m // tm,), in_specs=[row, row, vec, vec], out_specs=(row, row),
        out_shape=(jax.ShapeDtypeStruct((m, d), F32), jax.ShapeDtypeStruct((m, d), BF16)),
        compiler_params=_cparams(("parallel",)), name="post_norm",
    )(y, x, g_post.reshape(1, d), g_next.reshape(1, d))


def _mm_kernel(x_ref, w_ref, o_ref, *, sq_relu):
    r = jnp.dot(x_ref[...], w_ref[...], preferred_element_type=F32)
    if sq_relu:
        r = jnp.square(jnp.maximum(r, 0.0))
    o_ref[...] = r.astype(o_ref.dtype)


def _mm_acc_kernel(x_ref, w_ref, o_ref, acc_ref, *, nk):
    k = pl.program_id(2)

    @pl.when(k == 0)
    def _():
        acc_ref[...] = jnp.zeros_like(acc_ref)

    acc_ref[...] += jnp.dot(x_ref[...], w_ref[...], preferred_element_type=F32)

    @pl.when(k == nk - 1)
    def _():
        o_ref[...] = acc_ref[...].astype(o_ref.dtype)


def matmul(x, w, out_dtype, tm, tn, tk=None, sq_relu=False, name="matmul"):
    m, kd = x.shape
    _, n = w.shape
    assert m % tm == 0 and n % tn == 0
    if tk is None or tk == kd:
        return pl.pallas_call(
            functools.partial(_mm_kernel, sq_relu=sq_relu),
            grid=(m // tm, n // tn),
            in_specs=[pl.BlockSpec((tm, kd), lambda i, j: (i, 0)),
                      pl.BlockSpec((kd, tn), lambda i, j: (0, j))],
            out_specs=pl.BlockSpec((tm, tn), lambda i, j: (i, j)),
            out_shape=jax.ShapeDtypeStruct((m, n), out_dtype),
            compiler_params=_cparams(("parallel", "parallel")),
            name=name,
        )(x, w)
    nk = kd // tk
    return pl.pallas_call(
        functools.partial(_mm_acc_kernel, nk=nk),
        grid=(m // tm, n // tn, nk),
        in_specs=[pl.BlockSpec((tm, tk), lambda i, j, k: (i, k)),
                  pl.BlockSpec((tk, tn), lambda i, j, k: (k, j))],
        out_specs=pl.BlockSpec((tm, tn), lambda i, j, k: (i, j)),
        out_shape=jax.ShapeDtypeStruct((m, n), out_dtype),
        scratch_shapes=[pltpu.VMEM((tm, tn), F32)],
        compiler_params=_cparams(("parallel", "parallel", "arbitrary")),
        name=name,
    )(x, w)


def _mm_ws_kernel(x_ref, w_ref, o_ref, wb_ref, *, w_rows_are_outputs, sq_relu):
    @pl.when(pl.program_id(1) == 0)
    def _():
        wb_ref[...] = w_ref[...].reshape(wb_ref.shape).astype(wb_ref.dtype)

    if w_rows_are_outputs:
        r = lax.dot_general(x_ref[...], wb_ref[...], (((1,), (1,)), ((), ())), preferred_element_type=F32)
    else:
        r = jnp.dot(x_ref[...], wb_ref[...], preferred_element_type=F32)
    if sq_relu:
        r = jnp.square(jnp.maximum(r, 0.0))
    o_ref[...] = r.astype(o_ref.dtype)


def matmul_ws(x, w, l, out_dtype, tm, tn, n_out, w_row=None, sq_relu=False, name="matmul_ws"):
    m, kd = x.shape
    assert m % tm == 0 and n_out % tn == 0
    if w_row is None:
        w_spec = pl.BlockSpec((None, kd, tn), lambda j, i: (l, 0, j))
        wb_shape = (kd, tn)
    else:
        w_spec = pl.BlockSpec((pl.Element(1), pl.Element(tn), pl.Element(kd)), lambda j, i: (l, pl.multiple_of(w_row(j), 8), 0))
        wb_shape = (tn, kd)
    return pl.pallas_call(
        functools.partial(_mm_ws_kernel, w_rows_are_outputs=w_row is not None, sq_relu=sq_relu),
        grid=(n_out // tn, m // tm),
        in_specs=[pl.BlockSpec((tm, kd), lambda j, i: (i, 0)), w_spec],
        out_specs=pl.BlockSpec((tm, tn), lambda j, i: (i, j)),
        out_shape=jax.ShapeDtypeStruct((m, n_out), out_dtype),
        scratch_shapes=[pltpu.VMEM(wb_shape, BF16)],
        compiler_params=_cparams(("parallel", "arbitrary")),
        name=name,
    )(x, w)


def _split_bf16(x):
    hi = x.astype(BF16)
    lo = (x - hi.astype(F32)).astype(BF16)
    return hi, lo


def _log_sigmoid_pair(z):
    sp = jnp.log(1.0 + jnp.exp(-jnp.abs(z)))
    return jnp.minimum(z, 0.0) - sp, -jnp.maximum(z, 0.0) - sp


def _dot_nt(a, b):
    return lax.dot_general(a, b, (((1,), (1,)), ((), ())), preferred_element_type=F32)


def _dot_tn(a, b):
    return lax.dot_general(a, b, (((0,), (0,)), ((), ())), preferred_element_type=F32)


def _sb_kernel(q_ref, k_ref, v_ref, o_ref, *, tq, nh):
    qi = pl.program_id(2)
    dh = SB_DH
    scale = dh ** -0.5
    nsub = tq // LANE
    rr = lax.broadcasted_iota(jnp.int32, (LANE, 2 * LANE), 0)
    cc = lax.broadcasted_iota(jnp.int32, (LANE, 2 * LANE), 1)
    cum_rhs = ((cc >= LANE) | (rr > cc)).astype(BF16)
    cum_rhs = jnp.concatenate([cum_rhs, cum_rhs], axis=0)
    rows = lax.broadcasted_iota(jnp.int32, (tq, tq), 0)
    cols = lax.broadcasted_iota(jnp.int32, (tq, tq), 1)
    strict = cols < rows

    def tile(j, state, diag):
        start = pl.multiple_of(j * tq, tq)
        new_state = []
        for h in range(nh):
            run, acc = state[h]
            hs = slice(h * dh, (h + 1) * dh)
            z = _dot_nt(q_ref[:, hs], k_ref[pl.ds(start, tq), hs]) * scale
            log_beta, log_1m = _log_sigmoid_pair(z)
            if diag:
                log_1m = jnp.where(strict, log_1m, 0.0)
            hi, lo = _split_bf16(log_1m)
            after = [None] * nsub
            for sb in reversed(range(nsub)):
                cs = slice(sb * LANE, (sb + 1) * LANE)
                r2 = jnp.dot(jnp.concatenate([hi[:, cs], lo[:, cs]], axis=1), cum_rhs,
                             preferred_element_type=F32)
                after[sb] = r2[:, :LANE] + run
                run = run + r2[:, LANE:]
            w = jnp.exp(log_beta + jnp.concatenate(after, axis=1))
            if diag:
                w = jnp.where(strict, w, 0.0)
            acc = acc + jnp.dot(w.astype(BF16), v_ref[pl.ds(start, tq), hs], preferred_element_type=F32)
            new_state.append((run, acc))
        return tuple(new_state)

    zero = jnp.zeros((tq, LANE), F32)
    state = tile(qi, tuple((zero, zero) for _ in range(nh)), True)
    state = lax.fori_loop(1, qi + 1, lambda jj, s: tile(qi - jj, s, False), state)
    for h in range(nh):
        o_ref[:, h * dh:(h + 1) * dh] = state[h][1].astype(o_ref.dtype)


def stickbreaking(p_sb, batch, seq, tq=256, nh=4):
    assert SB_DH == LANE
    nq = seq // tq
    ng = SB_HEADS // nh
    wb = nh * SB_DH
    return pl.pallas_call(
        functools.partial(_sb_kernel, tq=tq, nh=nh),
        grid=(batch, ng, nq),
        in_specs=[pl.BlockSpec((tq, wb), lambda b, g, i: (b * nq + i, g)),
                  pl.BlockSpec((seq, wb), lambda b, g, i: (b, ng + g)),
                  pl.BlockSpec((seq, wb), lambda b, g, i: (b, 2 * ng + g))],
        out_specs=pl.BlockSpec((tq, wb), lambda b, g, i: (b * nq + i, g)),
        out_shape=jax.ShapeDtypeStruct((batch * seq, SB_HEADS * SB_DH), BF16),
        compiler_params=_cparams(("parallel", "parallel", "arbitrary")),
        name="stickbreaking",
    )(p_sb, p_sb, p_sb)


def _gla_kernel(q_ref, k_ref, v_ref, g_ref, small_ref, up_ref, bias_ref, gain_ref, o_ref, st_ref):
    c = GLA_CHUNK

    @pl.when(pl.program_id(1) == 0)
    def _():
        st_ref[...] = jnp.zeros_like(st_ref)

    a_hi, a_lo = _split_bf16(small_ref[:, GA_LO:GA_LO + GLA_RANK])
    u_hi, u_lo = _split_bf16(up_ref[...])
    x = (jnp.dot(a_hi, u_hi, preferred_element_type=F32) + jnp.dot(a_hi, u_lo, preferred_element_type=F32)
         + jnp.dot(a_lo, u_hi, preferred_element_type=F32)) + bias_ref[...]
    log_a = _log_sigmoid_pair(x)[0] / GLA_TAU
    r_i = lax.broadcasted_iota(jnp.int32, (c, c), 0)
    c_i = lax.broadcasted_iota(jnp.int32, (c, c), 1)
    tril = (c_i <= r_i).astype(BF16)
    g_hi, g_lo = _split_bf16(log_a)
    b_all = jnp.dot(tril, g_hi, preferred_element_type=F32) + jnp.dot(tril, g_lo, preferred_element_type=F32)

    sub = GLA_SUB
    lane = lax.broadcasted_iota(jnp.int32, (sub, LANE), 1)
    jrow = lax.broadcasted_iota(jnp.int32, (sub, LANE), 0)
    gain = gain_ref[...]

    for h in range(GLA_HEADS):
        b = b_all[:, h * GLA_DK:(h + 1) * GLA_DK]
        q = q_ref[:, h * GLA_DK:(h + 1) * GLA_DK] * (GLA_DK ** -0.5)
        k = k_ref[:, h * GLA_DK:(h + 1) * GLA_DK]
        v = v_ref[:, h * GLA_DV:(h + 1) * GLA_DV].astype(BF16)
        st = st_ref[h]
        o_inter = _dot_nt((q * jnp.exp(b)).astype(BF16), st.astype(BF16))

        o_rows = []
        for blk in range(c // sub):
            lo_r, hi_r = blk * sub, (blk + 1) * sub
            b_blk, q_blk, k_blk = b[lo_r:hi_r], q[lo_r:hi_r], k[lo_r:hi_r]
            o_blk = o_inter[lo_r:hi_r]
            if blk > 0:
                b_first = b[lo_r:lo_r + 1]
                qt = (q_blk * jnp.exp(b_blk - b_first)).astype(BF16)
                kt = (k[:lo_r] * jnp.exp(b_first - b[:lo_r])).astype(BF16)
                a_off = _dot_nt(qt, kt)
                o_blk = o_blk + jnp.dot(a_off.astype(BF16), v[:lo_r], preferred_element_type=F32)
            a_t = jnp.zeros((sub, LANE), F32)
            for i in range(sub):
                e = jnp.exp(jnp.minimum(b_blk[i:i + 1] - b_blk, 0.0))
                col = jnp.sum(q_blk[i:i + 1] * k_blk * e, axis=-1, keepdims=True)
                a_t = jnp.where(lane == i, col, a_t)
            a_t = jnp.where(jrow <= lane, a_t, 0.0)
            o_diag = _dot_tn(a_t.astype(BF16), v[lo_r:hi_r])
            o_rows.append(o_blk + o_diag[:sub])
        o = jnp.concatenate(o_rows, axis=0)

        b_last = b[c - 1:c]
        kd = (k * jnp.exp(b_last - b)).astype(BF16)
        st_ref[h] = st * jnp.exp(b_last) + _dot_tn(v, kd)

        o = o * lax.rsqrt(jnp.mean(o * o, axis=-1, keepdims=True) + EPS) * gain
        g = g_ref[:, h * GLA_DV:(h + 1) * GLA_DV]
        o = o * (g / (1.0 + jnp.exp(-g)))
        o_ref[:, h * GLA_DV:(h + 1) * GLA_DV] = o.astype(o_ref.dtype)


def gla(p_gla, p_small, gate_up, gate_bias, head_gain, batch, seq):
    c = GLA_CHUNK
    n = seq // c
    wq, wv = GLA_HEADS * GLA_DK, GLA_HEADS * GLA_DV
    return pl.pallas_call(
        _gla_kernel,
        grid=(batch, n),
        in_specs=[pl.BlockSpec((c, wq), lambda b, i: (b * n + i, 0)),
                  pl.BlockSpec((c, wq), lambda b, i: (b * n + i, 1)),
                  pl.BlockSpec((c, wv), lambda b, i: (b * n + i, 1)),
                  pl.BlockSpec((c, wv), lambda b, i: (b * n + i, 2)),
                  pl.BlockSpec((c, LANE), lambda b, i: (b * n + i, 0)),
                  pl.BlockSpec((GLA_RANK, wq), lambda b, i: (0, 0)),
                  pl.BlockSpec((1, wq), lambda b, i: (0, 0)),
                  pl.BlockSpec((1, GLA_DV), lambda b, i: (0, 0))],
        out_specs=pl.BlockSpec((c, wv), lambda b, i: (b * n + i, 0)),
        out_shape=jax.ShapeDtypeStruct((batch * seq, wv), BF16),
        scratch_shapes=[pltpu.VMEM((GLA_HEADS, GLA_DV, GLA_DK), F32)],
        compiler_params=_cparams(("parallel", "arbitrary")),
        name="gla",
    )(p_gla, p_gla, p_gla, p_gla, p_small, gate_up, gate_bias.reshape(1, wq), head_gain.reshape(1, GLA_DV))


def _bias_kernel(bucket_ref, rel_ref, o_ref):
    for h in range(DSA_HEADS):
        for off in range(3):
            bk = bucket_ref[off]

            def body(n, acc, bk=bk, h=h):
                return jnp.where(bk == n, rel_ref[n, h], acc)

            o_ref[h, off] = lax.fori_loop(0, N_BUCKETS, body, jnp.zeros(bk.shape, F32))


def _rel_bucket(dist):
    max_exact = N_BUCKETS // 2
    d = jnp.maximum(dist, 1).astype(F32)
    large = max_exact + (jnp.log(d / max_exact) / math.log(MAX_DISTANCE / max_exact)
                         * (N_BUCKETS - max_exact)).astype(jnp.int32)
    large = jnp.minimum(large, N_BUCKETS - 1)
    return jnp.where(dist < max_exact, dist, large)


def bias_tiles(rel_bias):
    s = jnp.arange(Q_BLOCK, dtype=jnp.int32)[:, None]
    t = jnp.arange(Q_BLOCK, dtype=jnp.int32)[None, :]
    dist = jnp.stack([off * Q_BLOCK + t - s for off in range(3)])
    bucket = _rel_bucket(jnp.maximum(dist, 0))
    return pl.pallas_call(
        _bias_kernel,
        in_specs=[pl.BlockSpec(memory_space=pltpu.VMEM), pl.BlockSpec(memory_space=pltpu.SMEM)],
        out_specs=pl.BlockSpec(memory_space=pltpu.VMEM),
        out_shape=jax.ShapeDtypeStruct((DSA_HEADS, 3, Q_BLOCK, Q_BLOCK), F32),
        name="bias_tiles",
    )(bucket, rel_bias)


DSA_CHUNK = 2 * Q_BLOCK
NEG_BIG = -1e30


def _dsa_kernel(row_ref, smq_ref, sma_ref, dk_ref, dv_ref, bias_ref, o_ref,
                key_ref, neg_ref, q2_ref, acc_ref, *, topk):
    qb, ck = Q_BLOCK, DSA_CHUNK
    i = pl.program_id(1)
    nck = (i + 2) // 2
    s_loc = lax.broadcasted_iota(jnp.int32, (ck, qb), 0)
    t_loc = lax.broadcasted_iota(jnp.int32, (ck, qb), 1)
    lane_k = lax.broadcasted_iota(jnp.int32, (1, LANE), 1)

    def visible(r0):
        return (r0 + s_loc) <= (i * qb + t_loc)

    w_t = smq_ref[...].T * ((IDX_HEADS ** -0.5) * (IDX_DIM ** -0.5))
    for pr in range(IDX_HEADS // 2):
        qpair = row_ref[:, DSA_IQ + pr * LANE:DSA_IQ + (pr + 1) * LANE]
        zero = jnp.zeros_like(qpair)
        q2_ref[pr, :qb, :] = jnp.where(lane_k < IDX_DIM, qpair, zero)
        q2_ref[pr, qb:, :] = jnp.where(lane_k >= IDX_DIM, qpair, zero)

    def score_chunk(c, carry):
        r0 = pl.multiple_of(c * ck, ck)
        sm = sma_ref[pl.ds(r0, ck), :]
        k_dup = jnp.where(lane_k < IDX_DIM, pltpu.roll(sm, LANE - IK_LO, 1),
                          pltpu.roll(sm, IDX_DIM - IK_LO, 1)).astype(BF16)
        acc = jnp.zeros((ck, qb), F32)
        for pr in range(IDX_HEADS // 2):
            s2 = jnp.maximum(_dot_nt(k_dup, q2_ref[pr]), 0.0)
            he, ho = IW_LO + 2 * pr, IW_LO + 2 * pr + 1
            acc = acc + w_t[he:he + 1] * s2[:, :qb] + w_t[ho:ho + 1] * s2[:, qb:]
        val = jnp.where(visible(r0), acc + 0.0, -jnp.inf)
        bits = pltpu.bitcast(val, jnp.int32)
        key_ref[pl.ds(r0, ck), :] = jnp.where(bits < 0, bits ^ jnp.int32(0x7FFFFFFF), bits)
        return carry

    lax.fori_loop(0, nck, score_chunk, 0)

    kf = float(topk)

    def count(pred):
        def body(c, acc):
            r0 = pl.multiple_of(c * ck, ck)
            return acc + jnp.where(pred(key_ref[pl.ds(r0, ck), :]), 1.0, 0.0)
        acc = lax.fori_loop(0, nck, body, jnp.zeros((ck, qb), F32))
        return jnp.sum(acc, axis=0, keepdims=True)

    int_min = jnp.int32(-2 ** 31)
    m0 = jnp.where(count(lambda kk: kk >= 0) >= kf, jnp.int32(0), int_min)

    def search(n, m):
        cand = m | lax.shift_left(jnp.int32(1), jnp.int32(30) - n)
        return jnp.where(count(lambda kk: kk >= cand) >= kf, cand, m)

    kth = lax.fori_loop(0, 31, search, m0)
    need = kf - count(lambda kk: kk > kth)

    tri_r = lax.broadcasted_iota(jnp.int32, (ck, ck), 0)
    tri_c = lax.broadcasted_iota(jnp.int32, (ck, ck), 1)
    tri = (tri_c <= tri_r).astype(BF16)

    def select_chunk(c, seen):
        r0 = pl.multiple_of(c * ck, ck)
        kblk = key_ref[pl.ds(r0, ck), :]
        eq = jnp.where(kblk == kth, 1.0, 0.0)
        rank = jnp.dot(tri, eq.astype(BF16), preferred_element_type=F32) + seen
        keep = jnp.where(kblk > kth, 1.0, jnp.where(rank <= need, eq, 0.0))
        keep = jnp.where(visible(r0), keep, 0.0)
        neg_ref[pl.ds(r0, ck), :] = jnp.where(keep > 0.0, 0.0, -jnp.inf)
        return seen + jnp.sum(eq, axis=0, keepdims=True)

    lax.fori_loop(0, nck, select_chunk, jnp.zeros((1, qb), F32))

    acc_ref[...] = jnp.zeros_like(acc_ref)
    scale = DSA_DH ** -0.5

    def attend_chunk(c, ml):
        r0 = pl.multiple_of(c * ck, ck)
        kc = dk_ref[pl.ds(r0, ck), :]
        vc = dv_ref[pl.ds(r0, ck), :]
        neg = neg_ref[pl.ds(r0, ck), :]
        off0 = jnp.clip(i - 2 * c, 0, 2)
        off1 = jnp.clip(i - 2 * c - 1, 0, 2)
        new_ml = []
        for h in range(DSA_HEADS):
            m_old, l_old = ml[h]
            bias = jnp.concatenate([bias_ref[h, off0], bias_ref[h, off1]], axis=0)
            qh = row_ref[:, DSA_Q + h * DSA_DH:DSA_Q + (h + 1) * DSA_DH]
            lg = _dot_nt(kc, qh) * scale + bias + neg
            m_new = jnp.maximum(m_old, jnp.max(lg, axis=0, keepdims=True))
            alpha = jnp.exp(m_old - m_new)
            pexp = jnp.exp(lg - m_new)
            acc_ref[h] = alpha * acc_ref[h] + _dot_tn(vc, pexp.astype(BF16))
            new_ml.append((m_new, alpha * l_old + jnp.sum(pexp, axis=0, keepdims=True)))
        return tuple(new_ml)

    ml0 = tuple((jnp.full((1, qb), NEG_BIG, F32), jnp.zeros((1, qb), F32)) for _ in range(DSA_HEADS))
    ml = lax.fori_loop(0, nck, attend_chunk, ml0)
    for h in range(DSA_HEADS):
        o_t = acc_ref[h] / ml[h][1]
        o_ref[:, h * DSA_DH:(h + 1) * DSA_DH] = o_t.T.astype(o_ref.dtype)


def dsa(p_dsa, p_small, btiles, batch, seq):
    qb = Q_BLOCK
    nq = seq // qb
    assert seq % DSA_CHUNK == 0 and DSA_DH == LANE
    topk = min(TOPK_MAX, seq // 4)
    wq = DSA_HEADS * DSA_DH
    return pl.pallas_call(
        functools.partial(_dsa_kernel, topk=topk),
        grid=(batch, nq),
        in_specs=[pl.BlockSpec((qb, DSA_W), lambda b, i: (b * nq + i, 0)),
                  pl.BlockSpec((qb, LANE), lambda b, i: (b * nq + i, 1)),
                  pl.BlockSpec((seq, LANE), lambda b, i: (b, 1)),
                  pl.BlockSpec((seq, DSA_DH), lambda b, i: (b, DSA_K // DSA_DH)),
                  pl.BlockSpec((seq, DSA_DH), lambda b, i: (b, DSA_V // DSA_DH)),
                  pl.BlockSpec((DSA_HEADS, 3, qb, qb), lambda b, i: (0, 0, 0, 0))],
        out_specs=pl.BlockSpec((qb, wq), lambda b, i: (b * nq + i, 0)),
        out_shape=jax.ShapeDtypeStruct((batch * seq, wq), BF16),
        scratch_shapes=[pltpu.VMEM((seq, qb), jnp.int32), pltpu.VMEM((seq, qb), F32),
                        pltpu.VMEM((IDX_HEADS // 2, 2 * qb, LANE), BF16),
                        pltpu.VMEM((DSA_HEADS, DSA_DH, qb), F32)],
        compiler_params=_cparams(("parallel", "arbitrary")),
        name="dsa",
    )(p_dsa, p_small, p_small, p_dsa, p_dsa, btiles)


def _merge_kernel(oa_ref, ob_ref, oc_ref, ga_ref, gb_ref, gc_ref, wa_ref, wb_ref, wc_ref, o_ref):
    acc = None
    for o_r, g_r, w_r in ((oa_ref, ga_ref, wa_ref), (ob_ref, gb_ref, wb_ref), (oc_ref, gc_ref, wc_ref)):
        gate = 1.0 / (1.0 + jnp.exp(-g_r[...]))
        term = gate * jnp.dot(o_r[...], w_r[...], preferred_element_type=F32)
        acc = term if acc is None else acc + term
    o_ref[...] = acc.astype(o_ref.dtype)


def gated_merge(o_a, o_b, o_c, p_gates, w_branch, tm=512, tn=1024):
    m = o_a.shape[0]
    nj = D_MODEL // tn
    o_spec = pl.BlockSpec((tm, BRANCH_WIDTH), lambda i, j: (i, 0))

    def g_spec(br):
        return pl.BlockSpec((tm, tn), lambda i, j: (i, br * nj + j))

    def w_spec(br):
        return pl.BlockSpec((BRANCH_WIDTH, tn), lambda i, j: (br, j))

    return pl.pallas_call(
        _merge_kernel,
        grid=(m // tm, nj),
        in_specs=[o_spec, o_spec, o_spec, g_spec(0), g_spec(1), g_spec(2), w_spec(0), w_spec(1), w_spec(2)],
        out_specs=pl.BlockSpec((tm, tn), lambda i, j: (i, j)),
        out_shape=jax.ShapeDtypeStruct((m, D_MODEL), BF16),
        compiler_params=_cparams(("parallel", "parallel")),
        name="gated_merge",
    )(o_a, o_b, o_c, p_gates, p_gates, p_gates, w_branch, w_branch, w_branch)


def mixer(h, x, l, btiles, norm_mix_post, norm_mlp_pre, w_in_t, gla_gate_up, gla_gate_bias, gla_head_gain,
          w_branch, w_out, batch, seq):
    def proj(src, out_dtype, name):
        off, width = src
        return matmul_ws(h, w_in_t, l, out_dtype, 1024, PROJ_TN, width,
                         w_row=lambda j: off + j * PROJ_TN, name=name)

    p_gla = proj(SRC_GLA, F32, "proj_gla")
    p_dsa = proj(SRC_DSA, BF16, "proj_dsa")
    p_sb = proj(SRC_SB, BF16, "proj_sb")
    p_gates = proj(SRC_GATES, F32, "proj_gates")
    p_small = matmul_ws(h, w_in_t, l, F32, 1024, LANE, 2 * LANE,
                        w_row=lambda j: SMALL_ROWS[0] + j * (SMALL_ROWS[1] - SMALL_ROWS[0]), name="proj_small")
    o_a = gla(p_gla, p_small, gla_gate_up[l], gla_gate_bias[l], gla_head_gain[l], batch, seq)
    o_b = dsa(p_dsa, p_small, btiles, batch, seq)
    o_c = stickbreaking(p_sb, batch, seq)
    wb = cast_layer(w_branch.reshape(DEPTH, N_BRANCHES * BRANCH_WIDTH, D_MODEL), l)
    merged = gated_merge(o_a, o_b, o_c, p_gates, wb)
    y = matmul_ws(merged, w_out, l, F32, 1024, 512, D_MODEL, name="out_proj")
    return post_norm_residual(y, x, norm_mix_post[l], norm_mlp_pre[l])


def kernel(x, rel_bias, norm_mix_pre, norm_mix_post, norm_mlp_pre, norm_mlp_post, w_in, gla_gate_up,
           gla_gate_bias, gla_head_gain, w_branch, w_out, w_mlp_up, w_mlp_down):
    batch, seq, d = x.shape
    xf = x.reshape(batch * seq, d)
    btiles = bias_tiles(rel_bias)
    h = rmsnorm_cast(xf, norm_mix_pre[0])
    w_in_t = jnp.swapaxes(w_in, 1, 2)
    for l in range(DEPTH):
        xf, h = mixer(h, xf, l, btiles, norm_mix_post, norm_mlp_pre, w_in_t, gla_gate_up, gla_gate_bias,
                      gla_head_gain, w_branch, w_out, batch, seq)
        u = matmul_ws(h, w_mlp_up, l, BF16, 1024, 512, D_FF, sq_relu=True, name="mlp_up")
        y = matmul(u, cast_layer(w_mlp_down, l), F32, tm=1024, tn=1024, tk=4096, name="mlp_down")
        g_next = norm_mix_pre[l + 1] if l + 1 < DEPTH else None
        xf, h = post_norm_residual(y, xf, norm_mlp_post[l], g_next)
    return xf.reshape(batch, seq, d)
```

```python
import functools
import math

import jax
import jax.numpy as jnp
from jax import lax
from jax.experimental import pallas as pl
from jax.experimental.pallas import tpu as pltpu

F32 = jnp.float32
BF16 = jnp.bfloat16

D_MODEL = 4096
DEPTH = 2
N_BRANCHES = 3
BRANCH_WIDTH = 1024
GLA_HEADS, GLA_DK, GLA_DV, GLA_RANK, GLA_TAU, GLA_CHUNK = 4, 128, 256, 16, 16.0, 64
GLA_SUB = 16
DSA_HEADS, DSA_DH, IDX_HEADS, IDX_DIM, TOPK_MAX = 8, 128, 32, 64, 256
SB_HEADS, SB_DH = 8, 128
Q_BLOCK = 128
N_BUCKETS, MAX_DISTANCE = 32, 128
D_FF = 4 * D_MODEL
EPS = 1e-6
LANE = 128

WS_TM, WS_TN = 1024, 1024
SRC_GLA = (0, 3072)
SRC_GA = 3072
SRC_DSA = (3088, 3584)
SRC_IK = 6416
SRC_SB = (6512, 3072)
SRC_GATES = (9584, 3 * D_MODEL)
IN_COLS = SRC_GATES[0] + SRC_GATES[1]
DSA_Q, DSA_K, DSA_V, DSA_IQ, DSA_W = 0, 1024, 1152, 1280, 3328
SMALL_ROWS = (SRC_GA, SRC_IK - SRC_IK % LANE)
GA_LO, IK_LO, IW_LO = 0, SRC_IK % LANE, (SRC_IK + IDX_DIM) % LANE

VMEM_LIMIT = 56 * 1024 * 1024


def _cparams(sem):
    return pltpu.CompilerParams(dimension_semantics=sem, vmem_limit_bytes=VMEM_LIMIT)


def _cast_kernel(w_ref, o_ref):
    o_ref[...] = w_ref[...].astype(o_ref.dtype)


def cast_layer(w, l, tr=512, tc=4096):
    _, r, c = w.shape
    tc = min(tc, c)
    return pl.pallas_call(
        _cast_kernel,
        grid=(r // tr, c // tc),
        in_specs=[pl.BlockSpec((None, tr, tc), lambda i, j: (l, i, j))],
        out_specs=pl.BlockSpec((tr, tc), lambda i, j: (i, j)),
        out_shape=jax.ShapeDtypeStruct((r, c), BF16),
        compiler_params=_cparams(("parallel", "parallel")),
        name="cast_weight",
    )(w)


def _rms_kernel(x_ref, g_ref, o_ref):
    x = x_ref[...]
    y = x * lax.rsqrt(jnp.mean(x * x, axis=-1, keepdims=True) + EPS)
    o_ref[...] = (y * g_ref[...]).astype(o_ref.dtype)


def rmsnorm_cast(x, g, tm=256):
    m, d = x.shape
    return pl.pallas_call(
        _rms_kernel,
        grid=(m // tm,),
        in_specs=[pl.BlockSpec((tm, d), lambda i: (i, 0)), pl.BlockSpec((1, d), lambda i: (0, 0))],
        out_specs=pl.BlockSpec((tm, d), lambda i: (i, 0)),
        out_shape=jax.ShapeDtypeStruct((m, d), BF16),
        compiler_params=_cparams(("parallel",)),
        name="rmsnorm_cast",
    )(x, g.reshape(1, d))


def _post_kernel(y_ref, x_ref, gp_ref, gn_ref, xo_ref, ho_ref):
    y = y_ref[...]
    yn = y * lax.rsqrt(jnp.mean(y * y, axis=-1, keepdims=True) + EPS) * gp_ref[...]
    xn = x_ref[...] + yn
    xo_ref[...] = xn
    hn = xn * lax.rsqrt(jnp.mean(xn * xn, axis=-1, keepdims=True) + EPS) * gn_ref[...]
    ho_ref[...] = hn.astype(ho_ref.dtype)


def _post_last_kernel(y_ref, x_ref, gp_ref, xo_ref):
    y = y_ref[...]
    yn = y * lax.rsqrt(jnp.mean(y * y, axis=-1, keepdims=True) + EPS) * gp_ref[...]
    xo_ref[...] = x_ref[...] + yn


def post_norm_residual(y, x, g_post, g_next, tm=256):
    m, d = x.shape
    row = pl.BlockSpec((tm, d), lambda i: (i, 0))
    vec = pl.BlockSpec((1, d), lambda i: (0, 0))
    if g_next is None:
        return pl.pallas_call(
            _post_last_kernel, grid=(m // tm,), in_specs=[row, row, vec], out_specs=row,
            out_shape=jax.ShapeDtypeStruct((m, d), F32),
            compiler_params=_cparams(("parallel",)), name="post_last",
        )(y, x, g_post.reshape(1, d)), None
    return pl.pallas_call(
        _post_kernel, grid=(m // tm,), in_specs=[row, row, vec, vec], out_specs=(row, row),
        out_shape=(jax.ShapeDtypeStruct((m, d), F32), jax.ShapeDtypeStruct((m, d), BF16)),
        compiler_params=_cparams(("parallel",)), name="post_norm",
    )(y, x, g_post.reshape(1, d), g_next.reshape(1, d))


def _mm_kernel(x_ref, w_ref, o_ref, *, sq_relu):
    r = jnp.dot(x_ref[...], w_ref[...], preferred_element_type=F32)
    if sq_relu:
        r = jnp.square(jnp.maximum(r, 0.0))
    o_ref[...] = r.astype(o_ref.dtype)


def _mm_acc_kernel(x_ref, w_ref, o_ref, acc_ref, *, nk):
    k = pl.program_id(2)

    @pl.when(k == 0)
    def _():
        acc_ref[...] = jnp.zeros_like(acc_ref)

    acc_ref[...] += jnp.dot(x_ref[...], w_ref[...], preferred_element_type=F32)

    @pl.when(k == nk - 1)
    def _():
        o_ref[...] = acc_ref[...].astype(o_ref.dtype)


def matmul(x, w, out_dtype, tm, tn, tk=None, sq_relu=False, name="matmul"):
    m, kd = x.shape
    _, n = w.shape
    assert m % tm == 0 and n % tn == 0
    if tk is None or tk == kd:
        return pl.pallas_call(
            functools.partial(_mm_kernel, sq_relu=sq_relu),
            grid=(m // tm, n // tn),
            in_specs=[pl.BlockSpec((tm, kd), lambda i, j: (i, 0)),
                      pl.BlockSpec((kd, tn), lambda i, j: (0, j))],
            out_specs=pl.BlockSpec((tm, tn), lambda i, j: (i, j)),
            out_shape=jax.ShapeDtypeStruct((m, n), out_dtype),
            compiler_params=_cparams(("parallel", "parallel")),
            name=name,
        )(x, w)
    nk = kd // tk
    return pl.pallas_call(
        functools.partial(_mm_acc_kernel, nk=nk),
        grid=(m // tm, n // tn, nk),
        in_specs=[pl.BlockSpec((tm, tk), lambda i, j, k: (i, k)),
                  pl.BlockSpec((tk, tn), lambda i, j, k: (k, j))],
        out_specs=pl.BlockSpec((tm, tn), lambda i, j, k: (i, j)),
        out_shape=jax.ShapeDtypeStruct((m, n), out_dtype),
        scratch_shapes=[pltpu.VMEM((tm, tn), F32)],
        compiler_params=_cparams(("parallel", "parallel", "arbitrary")),
        name=name,
    )(x, w)


def _mm_ws_kernel(x_ref, w_hbm, o_ref, stage_ref, wb_ref, sem, *, l, tn, nj, w_row, sq_relu):
    j, i = pl.program_id(0), pl.program_id(1)
    w_rows_are_outputs = w_row is not None

    def tile_copy(jt):
        if w_rows_are_outputs:
            src = w_hbm.at[l, pl.ds(pl.multiple_of(w_row(jt), 8), tn), :]
        else:
            src = w_hbm.at[l, :, pl.ds(pl.multiple_of(jt * tn, LANE), tn)]
        return pltpu.make_async_copy(src, stage_ref, sem)

    @pl.when((j == 0) & (i == 0))
    def _():
        tile_copy(0).start()

    @pl.when(i == 0)
    def _():
        tile_copy(j).wait()
        wb_ref[...] = stage_ref[...].astype(wb_ref.dtype)

    @pl.when((i == 1) & (j + 1 < nj))
    def _():
        tile_copy(j + 1).start()

    if w_rows_are_outputs:
        r = lax.dot_general(x_ref[...], wb_ref[...], (((1,), (1,)), ((), ())), preferred_element_type=F32)
    else:
        r = jnp.dot(x_ref[...], wb_ref[...], preferred_element_type=F32)
    if sq_relu:
        r = jnp.square(jnp.maximum(r, 0.0))
    o_ref[...] = r.astype(o_ref.dtype)


def matmul_ws(x, w, l, out_dtype, tm, tn, n_out, w_row=None, sq_relu=False, name="matmul_ws"):
    m, kd = x.shape
    nj = n_out // tn
    assert m % tm == 0 and n_out % tn == 0 and m // tm >= 2
    w_shape = (kd, tn) if w_row is None else (tn, kd)
    return pl.pallas_call(
        functools.partial(_mm_ws_kernel, l=l, tn=tn, nj=nj, w_row=w_row, sq_relu=sq_relu),
        grid=(nj, m // tm),
        in_specs=[pl.BlockSpec((tm, kd), lambda j, i: (i, 0)), pl.BlockSpec(memory_space=pl.ANY)],
        out_specs=pl.BlockSpec((tm, tn), lambda j, i: (i, j)),
        out_shape=jax.ShapeDtypeStruct((m, n_out), out_dtype),
        scratch_shapes=[pltpu.VMEM(w_shape, F32), pltpu.VMEM(w_shape, BF16), pltpu.SemaphoreType.DMA(())],
        compiler_params=_cparams(("arbitrary", "arbitrary")),
        name=name,
    )(x, w)


def _split_bf16(x):
    hi = x.astype(BF16)
    lo = (x - hi.astype(F32)).astype(BF16)
    return hi, lo


def _log_sigmoid_pair(z):
    sp = jnp.log(1.0 + jnp.exp(-jnp.abs(z)))
    return jnp.minimum(z, 0.0) - sp, -jnp.maximum(z, 0.0) - sp


def _dot_nt(a, b):
    return lax.dot_general(a, b, (((1,), (1,)), ((), ())), preferred_element_type=F32)


def _dot_tn(a, b):
    return lax.dot_general(a, b, (((0,), (0,)), ((), ())), preferred_element_type=F32)


def _sb_kernel(q_ref, k_ref, v_ref, o_ref, *, tq, nh):
    qi = pl.program_id(2)
    dh = SB_DH
    scale = dh ** -0.5
    nsub = tq // LANE
    rr = lax.broadcasted_iota(jnp.int32, (LANE, 2 * LANE), 0)
    cc = lax.broadcasted_iota(jnp.int32, (LANE, 2 * LANE), 1)
    cum_rhs = ((cc >= LANE) | (rr > cc)).astype(BF16)
    cum_rhs = jnp.concatenate([cum_rhs, cum_rhs], axis=0)
    rows = lax.broadcasted_iota(jnp.int32, (tq, tq), 0)
    cols = lax.broadcasted_iota(jnp.int32, (tq, tq), 1)
    strict = cols < rows

    def tile(j, state, diag):
        start = pl.multiple_of(j * tq, tq)
        new_state = []
        for h in range(nh):
            run, acc = state[h]
            hs = slice(h * dh, (h + 1) * dh)
            z = _dot_nt(q_ref[:, hs], k_ref[pl.ds(start, tq), hs]) * scale
            log_beta, log_1m = _log_sigmoid_pair(z)
            if diag:
                log_1m = jnp.where(strict, log_1m, 0.0)
            hi, lo = _split_bf16(log_1m)
            after = [None] * nsub
            for sb in reversed(range(nsub)):
                cs = slice(sb * LANE, (sb + 1) * LANE)
                r2 = jnp.dot(jnp.concatenate([hi[:, cs], lo[:, cs]], axis=1), cum_rhs,
                             preferred_element_type=F32)
                after[sb] = r2[:, :LANE] + run
                run = run + r2[:, LANE:]
            w = jnp.exp(log_beta + jnp.concatenate(after, axis=1))
            if diag:
                w = jnp.where(strict, w, 0.0)
            acc = acc + jnp.dot(w.astype(BF16), v_ref[pl.ds(start, tq), hs], preferred_element_type=F32)
            new_state.append((run, acc))
        return tuple(new_state)

    zero = jnp.zeros((tq, LANE), F32)
    state = tile(qi, tuple((zero, zero) for _ in range(nh)), True)
    state = lax.fori_loop(1, qi + 1, lambda jj, s: tile(qi - jj, s, False), state)
    for h in range(nh):
        o_ref[:, h * dh:(h + 1) * dh] = state[h][1].astype(o_ref.dtype)


def stickbreaking(p_sb, batch, seq, tq=256, nh=4):
    assert SB_DH == LANE
    nq = seq // tq
    ng = SB_HEADS // nh
    wb = nh * SB_DH
    return pl.pallas_call(
        functools.partial(_sb_kernel, tq=tq, nh=nh),
        grid=(batch, ng, nq),
        in_specs=[pl.BlockSpec((tq, wb), lambda b, g, i: (b * nq + i, g)),
                  pl.BlockSpec((seq, wb), lambda b, g, i: (b, ng + g)),
                  pl.BlockSpec((seq, wb), lambda b, g, i: (b, 2 * ng + g))],
        out_specs=pl.BlockSpec((tq, wb), lambda b, g, i: (b * nq + i, g)),
        out_shape=jax.ShapeDtypeStruct((batch * seq, SB_HEADS * SB_DH), BF16),
        compiler_params=_cparams(("parallel", "parallel", "arbitrary")),
        name="stickbreaking",
    )(p_sb, p_sb, p_sb)


def _gla_kernel(q_ref, k_ref, v_ref, g_ref, small_ref, up_ref, bias_ref, gain_ref, o_ref, st_ref):
    c = GLA_CHUNK

    @pl.when(pl.program_id(1) == 0)
    def _():
        st_ref[...] = jnp.zeros_like(st_ref)

    a_hi, a_lo = _split_bf16(small_ref[:, GA_LO:GA_LO + GLA_RANK])
    u_hi, u_lo = _split_bf16(up_ref[...])
    x = (jnp.dot(a_hi, u_hi, preferred_element_type=F32) + jnp.dot(a_hi, u_lo, preferred_element_type=F32)
         + jnp.dot(a_lo, u_hi, preferred_element_type=F32)) + bias_ref[...]
    log_a = _log_sigmoid_pair(x)[0] / GLA_TAU
    r_i = lax.broadcasted_iota(jnp.int32, (c, c), 0)
    c_i = lax.broadcasted_iota(jnp.int32, (c, c), 1)
    tril = (c_i <= r_i).astype(BF16)
    g_hi, g_lo = _split_bf16(log_a)
    b_all = jnp.dot(tril, g_hi, preferred_element_type=F32) + jnp.dot(tril, g_lo, preferred_element_type=F32)

    sub = GLA_SUB
    lane = lax.broadcasted_iota(jnp.int32, (sub, LANE), 1)
    jrow = lax.broadcasted_iota(jnp.int32, (sub, LANE), 0)
    gain = gain_ref[...]

    for h in range(GLA_HEADS):
        b = b_all[:, h * GLA_DK:(h + 1) * GLA_DK]
        q = q_ref[:, h * GLA_DK:(h + 1) * GLA_DK] * (GLA_DK ** -0.5)
        k = k_ref[:, h * GLA_DK:(h + 1) * GLA_DK]
        v = v_ref[:, h * GLA_DV:(h + 1) * GLA_DV].astype(BF16)
        st = st_ref[h]
        o_inter = _dot_nt((q * jnp.exp(b)).astype(BF16), st.astype(BF16))

        o_rows = []
        for blk in range(c // sub):
            lo_r, hi_r = blk * sub, (blk + 1) * sub
            b_blk, q_blk, k_blk = b[lo_r:hi_r], q[lo_r:hi_r], k[lo_r:hi_r]
            o_blk = o_inter[lo_r:hi_r]
            if blk > 0:
                b_first = b[lo_r:lo_r + 1]
                qt = (q_blk * jnp.exp(b_blk - b_first)).astype(BF16)
                kt = (k[:lo_r] * jnp.exp(b_first - b[:lo_r])).astype(BF16)
                a_off = _dot_nt(qt, kt)
                o_blk = o_blk + jnp.dot(a_off.astype(BF16), v[:lo_r], preferred_element_type=F32)
            a_t = jnp.zeros((sub, LANE), F32)
            for i in range(sub):
                e = jnp.exp(jnp.minimum(b_blk[i:i + 1] - b_blk, 0.0))
                col = jnp.sum(q_blk[i:i + 1] * k_blk * e, axis=-1, keepdims=True)
                a_t = jnp.where(lane == i, col, a_t)
            a_t = jnp.where(jrow <= lane, a_t, 0.0)
            o_diag = _dot_tn(a_t.astype(BF16), v[lo_r:hi_r])
            o_rows.append(o_blk + o_diag[:sub])
        o = jnp.concatenate(o_rows, axis=0)

        b_last = b[c - 1:c]
        kd = (k * jnp.exp(b_last - b)).astype(BF16)
        st_ref[h] = st * jnp.exp(b_last) + _dot_tn(v, kd)

        o = o * lax.rsqrt(jnp.mean(o * o, axis=-1, keepdims=True) + EPS) * gain
        g = g_ref[:, h * GLA_DV:(h + 1) * GLA_DV]
        o = o * (g / (1.0 + jnp.exp(-g)))
        o_ref[:, h * GLA_DV:(h + 1) * GLA_DV] = o.astype(o_ref.dtype)


def gla(p_gla, p_small, gate_up, gate_bias, head_gain, batch, seq):
    c = GLA_CHUNK
    n = seq // c
    wq, wv = GLA_HEADS * GLA_DK, GLA_HEADS * GLA_DV
    return pl.pallas_call(
        _gla_kernel,
        grid=(batch, n),
        in_specs=[pl.BlockSpec((c, wq), lambda b, i: (b * n + i, 0)),
                  pl.BlockSpec((c, wq), lambda b, i: (b * n + i, 1)),
                  pl.BlockSpec((c, wv), lambda b, i: (b * n + i, 1)),
                  pl.BlockSpec((c, wv), lambda b, i: (b * n + i, 2)),
                  pl.BlockSpec((c, LANE), lambda b, i: (b * n + i, 0)),
                  pl.BlockSpec((GLA_RANK, wq), lambda b, i: (0, 0)),
                  pl.BlockSpec((1, wq), lambda b, i: (0, 0)),
                  pl.BlockSpec((1, GLA_DV), lambda b, i: (0, 0))],
        out_specs=pl.BlockSpec((c, wv), lambda b, i: (b * n + i, 0)),
        out_shape=jax.ShapeDtypeStruct((batch * seq, wv), BF16),
        scratch_shapes=[pltpu.VMEM((GLA_HEADS, GLA_DV, GLA_DK), F32)],
        compiler_params=_cparams(("parallel", "arbitrary")),
        name="gla",
    )(p_gla, p_gla, p_gla, p_gla, p_small, gate_up, gate_bias.reshape(1, wq), head_gain.reshape(1, GLA_DV))


def _bias_kernel(bucket_ref, rel_ref, o_ref):
    for h in range(DSA_HEADS):
        for off in range(3):
            bk = bucket_ref[off]

            def body(n, acc, bk=bk, h=h):
                return jnp.where(bk == n, rel_ref[n, h], acc)

            o_ref[h, off] = lax.fori_loop(0, N_BUCKETS, body, jnp.zeros(bk.shape, F32))


def _rel_bucket(dist):
    max_exact = N_BUCKETS // 2
    d = jnp.maximum(dist, 1).astype(F32)
    large = max_exact + (jnp.log(d / max_exact) / math.log(MAX_DISTANCE / max_exact)
                         * (N_BUCKETS - max_exact)).astype(jnp.int32)
    large = jnp.minimum(large, N_BUCKETS - 1)
    return jnp.where(dist < max_exact, dist, large)


def bias_tiles(rel_bias):
    s = jnp.arange(Q_BLOCK, dtype=jnp.int32)[:, None]
    t = jnp.arange(Q_BLOCK, dtype=jnp.int32)[None, :]
    dist = jnp.stack([off * Q_BLOCK + t - s for off in range(3)])
    bucket = _rel_bucket(jnp.maximum(dist, 0))
    return pl.pallas_call(
        _bias_kernel,
        in_specs=[pl.BlockSpec(memory_space=pltpu.VMEM), pl.BlockSpec(memory_space=pltpu.SMEM)],
        out_specs=pl.BlockSpec(memory_space=pltpu.VMEM),
        out_shape=jax.ShapeDtypeStruct((DSA_HEADS, 3, Q_BLOCK, Q_BLOCK), F32),
        name="bias_tiles",
    )(bucket, rel_bias)


DSA_CHUNK = 4 * Q_BLOCK
NEG_BIG = -1e30


def _dsa_kernel(row_ref, smq_ref, sma_ref, dk_ref, dv_ref, bias_ref, o_ref,
                key_ref, neg_ref, q2_ref, acc_ref, *, topk):
    qb, ck = Q_BLOCK, DSA_CHUNK
    i = pl.program_id(1)
    bpc = ck // qb
    nck = (i + bpc) // bpc
    s_loc = lax.broadcasted_iota(jnp.int32, (ck, qb), 0)
    t_loc = lax.broadcasted_iota(jnp.int32, (ck, qb), 1)
    lane_k = lax.broadcasted_iota(jnp.int32, (1, LANE), 1)

    def visible(r0):
        return (r0 + s_loc) <= (i * qb + t_loc)

    w_t = smq_ref[...].T * ((IDX_HEADS ** -0.5) * (IDX_DIM ** -0.5))
    for pr in range(IDX_HEADS // 2):
        qpair = row_ref[:, DSA_IQ + pr * LANE:DSA_IQ + (pr + 1) * LANE]
        zero = jnp.zeros_like(qpair)
        q2_ref[pr, :qb, :] = jnp.where(lane_k < IDX_DIM, qpair, zero)
        q2_ref[pr, qb:, :] = jnp.where(lane_k >= IDX_DIM, qpair, zero)

    def score_chunk(c, carry):
        r0 = pl.multiple_of(c * ck, ck)
        sm = sma_ref[pl.ds(r0, ck), :]
        k_dup = jnp.where(lane_k < IDX_DIM, pltpu.roll(sm, LANE - IK_LO, 1),
                          pltpu.roll(sm, IDX_DIM - IK_LO, 1)).astype(BF16)
        acc = jnp.zeros((ck, qb), F32)
        for pr in range(IDX_HEADS // 2):
            s2 = jnp.maximum(_dot_nt(k_dup, q2_ref[pr]), 0.0)
            he, ho = IW_LO + 2 * pr, IW_LO + 2 * pr + 1
            acc = acc + w_t[he:he + 1] * s2[:, :qb] + w_t[ho:ho + 1] * s2[:, qb:]
        val = jnp.where(visible(r0), acc + 0.0, -jnp.inf)
        bits = pltpu.bitcast(val, jnp.int32)
        key_ref[pl.ds(r0, ck), :] = jnp.where(bits < 0, bits ^ jnp.int32(0x7FFFFFFF), bits)
        return carry

    lax.fori_loop(0, nck, score_chunk, 0)

    kf = float(topk)

    def count(pred):
        def body(c, acc):
            r0 = pl.multiple_of(c * ck, ck)
            hit = jnp.where(pred(key_ref[pl.ds(r0, ck), :]), 1.0, 0.0)
            return acc + jnp.sum(hit.reshape(ck // 64, 64, qb), axis=0)
        acc = lax.fori_loop(0, nck, body, jnp.zeros((64, qb), F32))
        return jnp.sum(acc, axis=0, keepdims=True)

    int_min = jnp.int32(-2 ** 31)
    m0 = jnp.where(count(lambda kk: kk >= 0) >= kf, jnp.int32(0), int_min)

    def search(n, m):
        cand = m | lax.shift_left(jnp.int32(1), jnp.int32(30) - n)
        return jnp.where(count(lambda kk: kk >= cand) >= kf, cand, m)

    kth = lax.fori_loop(0, 31, search, m0)
    need = kf - count(lambda kk: kk > kth)

    tri_r = lax.broadcasted_iota(jnp.int32, (ck, ck), 0)
    tri_c = lax.broadcasted_iota(jnp.int32, (ck, ck), 1)
    tri = (tri_c <= tri_r).astype(BF16)

    def select_chunk(c, seen):
        r0 = pl.multiple_of(c * ck, ck)
        kblk = key_ref[pl.ds(r0, ck), :]
        eq = jnp.where(kblk == kth, 1.0, 0.0)
        rank = jnp.dot(tri, eq.astype(BF16), preferred_element_type=F32) + seen
        keep = jnp.where(kblk > kth, 1.0, jnp.where(rank <= need, eq, 0.0))
        keep = jnp.where(visible(r0), keep, 0.0)
        neg_ref[pl.ds(r0, ck), :] = jnp.where(keep > 0.0, 0.0, -jnp.inf)
        return seen + jnp.sum(eq, axis=0, keepdims=True)

    lax.fori_loop(0, nck, select_chunk, jnp.zeros((1, qb), F32))

    acc_ref[...] = jnp.zeros_like(acc_ref)
    scale = DSA_DH ** -0.5

    def attend_chunk(c, ml):
        r0 = pl.multiple_of(c * ck, ck)
        kc = dk_ref[pl.ds(r0, ck), :]
        vc = dv_ref[pl.ds(r0, ck), :]
        neg = neg_ref[pl.ds(r0, ck), :]
        offs = [jnp.clip(i - bpc * c - b, 0, 2) for b in range(bpc)]
        new_ml = []
        for h in range(DSA_HEADS):
            m_old, l_old = ml[h]
            bias = jnp.concatenate([bias_ref[h, off] for off in offs], axis=0)
            qh = row_ref[:, DSA_Q + h * DSA_DH:DSA_Q + (h + 1) * DSA_DH]
            lg = _dot_nt(kc, qh) * scale + bias + neg
            m_new = jnp.maximum(m_old, jnp.max(lg, axis=0, keepdims=True))
            alpha = jnp.exp(m_old - m_new)
            pexp = jnp.exp(lg - m_new)
            acc_ref[h] = alpha * acc_ref[h] + _dot_tn(vc, pexp.astype(BF16))
            new_ml.append((m_new, alpha * l_old + jnp.sum(pexp, axis=0, keepdims=True)))
        return tuple(new_ml)

    ml0 = tuple((jnp.full((1, qb), NEG_BIG, F32), jnp.zeros((1, qb), F32)) for _ in range(DSA_HEADS))
    ml = lax.fori_loop(0, nck, attend_chunk, ml0)
    for h in range(DSA_HEADS):
        o_t = acc_ref[h] / ml[h][1]
        o_ref[:, h * DSA_DH:(h + 1) * DSA_DH] = o_t.T.astype(o_ref.dtype)


def dsa(p_dsa, p_small, btiles, batch, seq):
    qb = Q_BLOCK
    nq = seq // qb
    assert seq % DSA_CHUNK == 0 and DSA_DH == LANE
    topk = min(TOPK_MAX, seq // 4)
    wq = DSA_HEADS * DSA_DH
    return pl.pallas_call(
        functools.partial(_dsa_kernel, topk=topk),
        grid=(batch, nq),
        in_specs=[pl.BlockSpec((qb, DSA_W), lambda b, i: (b * nq + i, 0)),
                  pl.BlockSpec((qb, LANE), lambda b, i: (b * nq + i, 1)),
                  pl.BlockSpec((seq, LANE), lambda b, i: (b, 1)),
                  pl.BlockSpec((seq, DSA_DH), lambda b, i: (b, DSA_K // DSA_DH)),
                  pl.BlockSpec((seq, DSA_DH), lambda b, i: (b, DSA_V // DSA_DH)),
                  pl.BlockSpec((DSA_HEADS, 3, qb, qb), lambda b, i: (0, 0, 0, 0))],
        out_specs=pl.BlockSpec((qb, wq), lambda b, i: (b * nq + i, 0)),
        out_shape=jax.ShapeDtypeStruct((batch * seq, wq), BF16),
        scratch_shapes=[pltpu.VMEM((seq, qb), jnp.int32), pltpu.VMEM((seq, qb), F32),
                        pltpu.VMEM((IDX_HEADS // 2, 2 * qb, LANE), BF16),
                        pltpu.VMEM((DSA_HEADS, DSA_DH, qb), F32)],
        compiler_params=_cparams(("parallel", "arbitrary")),
        name="dsa",
    )(p_dsa, p_small, p_small, p_dsa, p_dsa, btiles)


def _merge_kernel(oa_ref, ob_ref, oc_ref, ga_ref, gb_ref, gc_ref, wa_ref, wb_ref, wc_ref, o_ref):
    acc = None
    for o_r, g_r, w_r in ((oa_ref, ga_ref, wa_ref), (ob_ref, gb_ref, wb_ref), (oc_ref, gc_ref, wc_ref)):
        gate = 1.0 / (1.0 + jnp.exp(-g_r[...]))
        term = gate * jnp.dot(o_r[...], w_r[...], preferred_element_type=F32)
        acc = term if acc is None else acc + term
    o_ref[...] = acc.astype(o_ref.dtype)


def gated_merge(o_a, o_b, o_c, p_gates, w_branch, tm=512, tn=1024):
    m = o_a.shape[0]
    nj = D_MODEL // tn
    o_spec = pl.BlockSpec((tm, BRANCH_WIDTH), lambda i, j: (i, 0))

    def g_spec(br):
        return pl.BlockSpec((tm, tn), lambda i, j: (i, br * nj + j))

    def w_spec(br):
        return pl.BlockSpec((BRANCH_WIDTH, tn), lambda i, j: (br, j))

    return pl.pallas_call(
        _merge_kernel,
        grid=(m // tm, nj),
        in_specs=[o_spec, o_spec, o_spec, g_spec(0), g_spec(1), g_spec(2), w_spec(0), w_spec(1), w_spec(2)],
        out_specs=pl.BlockSpec((tm, tn), lambda i, j: (i, j)),
        out_shape=jax.ShapeDtypeStruct((m, D_MODEL), BF16),
        compiler_params=_cparams(("parallel", "parallel")),
        name="gated_merge",
    )(o_a, o_b, o_c, p_gates, p_gates, p_gates, w_branch, w_branch, w_branch)


def mixer(h, x, l, btiles, norm_mix_post, norm_mlp_pre, w_in_t, gla_gate_up, gla_gate_bias, gla_head_gain,
          w_branch, w_out, batch, seq):
    def proj(src, out_dtype, name, tn=WS_TN):
        off, width = src
        return matmul_ws(h, w_in_t, l, out_dtype, WS_TM, tn, width, w_row=lambda j: off + j * tn, name=name)

    p_gla = proj(SRC_GLA, F32, "proj_gla")
    p_dsa = proj(SRC_DSA, BF16, "proj_dsa", tn=WS_TN // 2)
    p_sb = proj(SRC_SB, BF16, "proj_sb")
    p_gates = proj(SRC_GATES, F32, "proj_gates")
    p_small = matmul_ws(h, w_in_t, l, F32, WS_TM, LANE, 2 * LANE,
                        w_row=lambda j: SMALL_ROWS[0] + j * (SMALL_ROWS[1] - SMALL_ROWS[0]), name="proj_small")
    o_a = gla(p_gla, p_small, gla_gate_up[l], gla_gate_bias[l], gla_head_gain[l], batch, seq)
    o_b = dsa(p_dsa, p_small, btiles, batch, seq)
    o_c = stickbreaking(p_sb, batch, seq)
    wb = cast_layer(w_branch.reshape(DEPTH, N_BRANCHES * BRANCH_WIDTH, D_MODEL), l)
    merged = gated_merge(o_a, o_b, o_c, p_gates, wb)
    y = matmul_ws(merged, w_out, l, F32, WS_TM, WS_TN, D_MODEL, name="out_proj")
    return post_norm_residual(y, x, norm_mix_post[l], norm_mlp_pre[l])


def kernel(x, rel_bias, norm_mix_pre, norm_mix_post, norm_mlp_pre, norm_mlp_post, w_in, gla_gate_up,
           gla_gate_bias, gla_head_gain, w_branch, w_out, w_mlp_up, w_mlp_down):
    batch, seq, d = x.shape
    xf = x.reshape(batch * seq, d)
    btiles = bias_tiles(rel_bias)
    h = rmsnorm_cast(xf, norm_mix_pre[0])
    w_in_t = jnp.swapaxes(w_in, 1, 2)
    for l in range(DEPTH):
        xf, h = mixer(h, xf, l, btiles, norm_mix_post, norm_mlp_pre, w_in_t, gla_gate_up, gla_gate_bias,
                      gla_head_gain, w_branch, w_out, batch, seq)
        u = matmul_ws(h, w_mlp_up, l, BF16, WS_TM, WS_TN, D_FF, sq_relu=True, name="mlp_up")
        y = matmul(u, cast_layer(w_mlp_down, l), F32, tm=1024, tn=1024, tk=4096, name="mlp_down")
        g_next = norm_mix_pre[l + 1] if l + 1 < DEPTH else None
        xf, h = post_norm_residual(y, xf, norm_mlp_post[l], g_next)
    return xf.reshape(batch, seq, d)
```

```python
import functools
import math

import jax
import jax.numpy as jnp
from jax import lax
from jax.experimental import pallas as pl
from jax.experimental.pallas import tpu as pltpu

F32 = jnp.float32
BF16 = jnp.bfloat16

D_MODEL = 4096
DEPTH = 2
N_BRANCHES = 3
BRANCH_WIDTH = 1024
GLA_HEADS, GLA_DK, GLA_DV, GLA_RANK, GLA_TAU, GLA_CHUNK = 4, 128, 256, 16, 16.0, 64
GLA_SUB = 16
DSA_HEADS, DSA_DH, IDX_HEADS, IDX_DIM, TOPK_MAX = 8, 128, 32, 64, 256
SB_HEADS, SB_DH = 8, 128
Q_BLOCK = 128
N_BUCKETS, MAX_DISTANCE = 32, 128
D_FF = 4 * D_MODEL
EPS = 1e-6
LANE = 128

WS_TM, WS_TN = 1024, 1024
SRC_GLA = (0, 3072)
SRC_GA = 3072
SRC_DSA = (3088, 3584)
SRC_IK = 6416
SRC_SB = (6512, 3072)
SRC_GATES = (9584, 3 * D_MODEL)
IN_COLS = SRC_GATES[0] + SRC_GATES[1]
DSA_Q, DSA_K, DSA_V, DSA_IQ, DSA_W = 0, 1024, 1152, 1280, 3328
SMALL_ROWS = (SRC_GA, SRC_IK - SRC_IK % LANE)
GA_LO, IK_LO, IW_LO = 0, SRC_IK % LANE, (SRC_IK + IDX_DIM) % LANE

VMEM_LIMIT = 60 * 1024 * 1024


def _cparams(sem):
    return pltpu.CompilerParams(dimension_semantics=sem, vmem_limit_bytes=VMEM_LIMIT)


def _rms_kernel(x_ref, g_ref, o_ref):
    x = x_ref[...]
    y = x * lax.rsqrt(jnp.mean(x * x, axis=-1, keepdims=True) + EPS)
    o_ref[...] = (y * g_ref[...]).astype(o_ref.dtype)


def rmsnorm_cast(x, g, tm=256):
    m, d = x.shape
    return pl.pallas_call(
        _rms_kernel,
        grid=(m // tm,),
        in_specs=[pl.BlockSpec((tm, d), lambda i: (i, 0)), pl.BlockSpec((1, d), lambda i: (0, 0))],
        out_specs=pl.BlockSpec((tm, d), lambda i: (i, 0)),
        out_shape=jax.ShapeDtypeStruct((m, d), BF16),
        compiler_params=_cparams(("parallel",)),
        name="rmsnorm_cast",
    )(x, g.reshape(1, d))


def _post_kernel(y_ref, x_ref, gp_ref, gn_ref, xo_ref, ho_ref):
    y = y_ref[...]
    yn = y * lax.rsqrt(jnp.mean(y * y, axis=-1, keepdims=True) + EPS) * gp_ref[...]
    xn = x_ref[...] + yn
    xo_ref[...] = xn
    hn = xn * lax.rsqrt(jnp.mean(xn * xn, axis=-1, keepdims=True) + EPS) * gn_ref[...]
    ho_ref[...] = hn.astype(ho_ref.dtype)


def _post_last_kernel(y_ref, x_ref, gp_ref, xo_ref):
    y = y_ref[...]
    yn = y * lax.rsqrt(jnp.mean(y * y, axis=-1, keepdims=True) + EPS) * gp_ref[...]
    xo_ref[...] = x_ref[...] + yn


def post_norm_residual(y, x, g_post, g_next, tm=256):
    m, d = x.shape
    row = pl.BlockSpec((tm, d), lambda i: (i, 0))
    vec = pl.BlockSpec((1, d), lambda i: (0, 0))
    if g_next is None:
        return pl.pallas_call(
            _post_last_kernel, grid=(m // tm,), in_specs=[row, row, vec], out_specs=row,
            out_shape=jax.ShapeDtypeStruct((m, d), F32),
            compiler_params=_cparams(("parallel",)), name="post_last",
        )(y, x, g_post.reshape(1, d)), None
    return pl.pallas_call(
        _post_kernel, grid=(m // tm,), in_specs=[row, row, vec, vec], out_specs=(row, row),
        out_shape=(jax.ShapeDtypeStruct((m, d), F32), jax.ShapeDtypeStruct((m, d), BF16)),
        compiler_params=_cparams(("parallel",)), name="post_norm",
    )(y, x, g_post.reshape(1, d), g_next.reshape(1, d))


def _mm_kernel(x_ref, w_ref, o_ref, *, sq_relu):
    r = jnp.dot(x_ref[...], w_ref[...], preferred_element_type=F32)
    if sq_relu:
        r = jnp.square(jnp.maximum(r, 0.0))
    o_ref[...] = r.astype(o_ref.dtype)


def _mm_acc_kernel(x_ref, w_ref, o_ref, acc_ref, *, nk):
    k = pl.program_id(2)

    @pl.when(k == 0)
    def _():
        acc_ref[...] = jnp.zeros_like(acc_ref)

    acc_ref[...] += jnp.dot(x_ref[...], w_ref[...], preferred_element_type=F32)

    @pl.when(k == nk - 1)
    def _():
        o_ref[...] = acc_ref[...].astype(o_ref.dtype)


def matmul(x, w, out_dtype, tm, tn, tk=None, sq_relu=False, name="matmul"):
    m, kd = x.shape
    _, n = w.shape
    assert m % tm == 0 and n % tn == 0
    if tk is None or tk == kd:
        return pl.pallas_call(
            functools.partial(_mm_kernel, sq_relu=sq_relu),
            grid=(m // tm, n // tn),
            in_specs=[pl.BlockSpec((tm, kd), lambda i, j: (i, 0)),
                      pl.BlockSpec((kd, tn), lambda i, j: (0, j))],
            out_specs=pl.BlockSpec((tm, tn), lambda i, j: (i, j)),
            out_shape=jax.ShapeDtypeStruct((m, n), out_dtype),
            compiler_params=_cparams(("parallel", "parallel")),
            name=name,
        )(x, w)
    nk = kd // tk
    return pl.pallas_call(
        functools.partial(_mm_acc_kernel, nk=nk),
        grid=(m // tm, n // tn, nk),
        in_specs=[pl.BlockSpec((tm, tk), lambda i, j, k: (i, k)),
                  pl.BlockSpec((tk, tn), lambda i, j, k: (k, j))],
        out_specs=pl.BlockSpec((tm, tn), lambda i, j, k: (i, j)),
        out_shape=jax.ShapeDtypeStruct((m, n), out_dtype),
        scratch_shapes=[pltpu.VMEM((tm, tn), F32)],
        compiler_params=_cparams(("parallel", "parallel", "arbitrary")),
        name=name,
    )(x, w)


def _mm_ws_kernel(*refs, l, tn, nj, w_row, sq_relu, has_side):
    if has_side:
        x_ref, w_hbm, side_ref, o_ref, side_o_ref, stage_ref, wb_ref, sem = refs
        side_o_ref[...] = side_ref[...].astype(side_o_ref.dtype)
    else:
        x_ref, w_hbm, o_ref, stage_ref, wb_ref, sem = refs
    j, i = pl.program_id(0), pl.program_id(1)
    w_rows_are_outputs = w_row is not None

    def tile_copy(jt):
        if w_rows_are_outputs:
            src = w_hbm.at[l, pl.ds(pl.multiple_of(w_row(jt), 8), tn), :]
        else:
            src = w_hbm.at[l, :, pl.ds(pl.multiple_of(jt * tn, LANE), tn)]
        return pltpu.make_async_copy(src, stage_ref, sem)

    @pl.when((j == 0) & (i == 0))
    def _():
        tile_copy(0).start()

    @pl.when(i == 0)
    def _():
        tile_copy(j).wait()
        wb_ref[...] = stage_ref[...].astype(wb_ref.dtype)

    @pl.when((i == 1) & (j + 1 < nj))
    def _():
        tile_copy(j + 1).start()

    if w_rows_are_outputs:
        r = lax.dot_general(x_ref[...], wb_ref[...], (((1,), (1,)), ((), ())), preferred_element_type=F32)
    else:
        r = jnp.dot(x_ref[...], wb_ref[...], preferred_element_type=F32)
    if sq_relu:
        r = jnp.square(jnp.maximum(r, 0.0))
    o_ref[...] = r.astype(o_ref.dtype)


def matmul_ws(x, w, l, out_dtype, tm, tn, n_out, w_row=None, sq_relu=False, side=None, name="matmul_ws"):
    m, kd = x.shape
    nj, ni = n_out // tn, m // tm
    assert m % tm == 0 and n_out % tn == 0 and ni >= 2
    w_shape = (kd, tn) if w_row is None else (tn, kd)
    in_specs = [pl.BlockSpec((tm, kd), lambda j, i: (i, 0)), pl.BlockSpec(memory_space=pl.ANY)]
    out_specs = pl.BlockSpec((tm, tn), lambda j, i: (i, j))
    out_shape = jax.ShapeDtypeStruct((m, n_out), out_dtype)
    args = (x, w)
    if side is not None:
        _, sr, sc = side.shape
        assert sr % (nj * ni * 16) == 0
        slab = sr // (nj * ni)
        in_specs.append(pl.BlockSpec((None, slab, sc), lambda j, i: (l, j * ni + i, 0)))
        out_specs = (out_specs, pl.BlockSpec((slab, sc), lambda j, i: (j * ni + i, 0)))
        out_shape = (out_shape, jax.ShapeDtypeStruct((sr, sc), BF16))
        args = (x, w, side)
    return pl.pallas_call(
        functools.partial(_mm_ws_kernel, l=l, tn=tn, nj=nj, w_row=w_row, sq_relu=sq_relu,
                          has_side=side is not None),
        grid=(nj, ni),
        in_specs=in_specs,
        out_specs=out_specs,
        out_shape=out_shape,
        scratch_shapes=[pltpu.VMEM(w_shape, F32), pltpu.VMEM(w_shape, BF16), pltpu.SemaphoreType.DMA(())],
        compiler_params=_cparams(("arbitrary", "arbitrary")),
        name=name,
    )(*args)


def _split_bf16(x):
    hi = x.astype(BF16)
    lo = (x - hi.astype(F32)).astype(BF16)
    return hi, lo


def _log_sigmoid_pair(z):
    sp = jnp.log(1.0 + jnp.exp(-jnp.abs(z)))
    return jnp.minimum(z, 0.0) - sp, -jnp.maximum(z, 0.0) - sp


def _dot_nt(a, b):
    return lax.dot_general(a, b, (((1,), (1,)), ((), ())), preferred_element_type=F32)


def _dot_tn(a, b):
    return lax.dot_general(a, b, (((0,), (0,)), ((), ())), preferred_element_type=F32)


def _sb_kernel(q_ref, k_ref, v_ref, o_ref, *, tq, nh):
    qi = pl.program_id(2)
    dh = SB_DH
    scale = dh ** -0.5
    nsub = tq // LANE
    rr = lax.broadcasted_iota(jnp.int32, (LANE, 2 * LANE), 0)
    cc = lax.broadcasted_iota(jnp.int32, (LANE, 2 * LANE), 1)
    cum_rhs = ((cc >= LANE) | (rr > cc)).astype(BF16)
    cum_rhs = jnp.concatenate([cum_rhs, cum_rhs], axis=0)
    rows = lax.broadcasted_iota(jnp.int32, (tq, tq), 0)
    cols = lax.broadcasted_iota(jnp.int32, (tq, tq), 1)
    strict = cols < rows

    def tile(j, state, diag):
        start = pl.multiple_of(j * tq, tq)
        new_state = []
        for h in range(nh):
            run, acc = state[h]
            hs = slice(h * dh, (h + 1) * dh)
            z = _dot_nt(q_ref[:, hs], k_ref[pl.ds(start, tq), hs]) * scale
            log_beta, log_1m = _log_sigmoid_pair(z)
            if diag:
                log_1m = jnp.where(strict, log_1m, 0.0)
            hi, lo = _split_bf16(log_1m)
            after = [None] * nsub
            for sb in reversed(range(nsub)):
                cs = slice(sb * LANE, (sb + 1) * LANE)
                r2 = jnp.dot(jnp.concatenate([hi[:, cs], lo[:, cs]], axis=1), cum_rhs,
                             preferred_element_type=F32)
                after[sb] = r2[:, :LANE] + run
                run = run + r2[:, LANE:]
            w = jnp.exp(log_beta + jnp.concatenate(after, axis=1))
            if diag:
                w = jnp.where(strict, w, 0.0)
            acc = acc + jnp.dot(w.astype(BF16), v_ref[pl.ds(start, tq), hs], preferred_element_type=F32)
            new_state.append((run, acc))
        return tuple(new_state)

    zero = jnp.zeros((tq, LANE), F32)
    state = tile(qi, tuple((zero, zero) for _ in range(nh)), True)
    state = lax.fori_loop(1, qi + 1, lambda jj, s: tile(qi - jj, s, False), state)
    for h in range(nh):
        o_ref[:, h * dh:(h + 1) * dh] = state[h][1].astype(o_ref.dtype)


def stickbreaking(p_sb, batch, seq, tq=256, nh=8):
    assert SB_DH == LANE
    nq = seq // tq
    ng = SB_HEADS // nh
    wb = nh * SB_DH
    return pl.pallas_call(
        functools.partial(_sb_kernel, tq=tq, nh=nh),
        grid=(batch, ng, nq),
        in_specs=[pl.BlockSpec((tq, wb), lambda b, g, i: (b * nq + i, g)),
                  pl.BlockSpec((seq, wb), lambda b, g, i: (b, ng + g)),
                  pl.BlockSpec((seq, wb), lambda b, g, i: (b, 2 * ng + g))],
        out_specs=pl.BlockSpec((tq, wb), lambda b, g, i: (b * nq + i, g)),
        out_shape=jax.ShapeDtypeStruct((batch * seq, SB_HEADS * SB_DH), BF16),
        compiler_params=_cparams(("parallel", "parallel", "arbitrary")),
        name="stickbreaking",
    )(p_sb, p_sb, p_sb)


def _gla_kernel(q_ref, k_ref, v_ref, g_ref, small_ref, up_ref, bias_ref, gain_ref, o_ref, st_ref):
    c = GLA_CHUNK

    @pl.when(pl.program_id(1) == 0)
    def _():
        st_ref[...] = jnp.zeros_like(st_ref)

    a_hi, a_lo = _split_bf16(small_ref[:, GA_LO:GA_LO + GLA_RANK])
    u_hi, u_lo = _split_bf16(up_ref[...])
    x = (jnp.dot(a_hi, u_hi, preferred_element_type=F32) + jnp.dot(a_hi, u_lo, preferred_element_type=F32)
         + jnp.dot(a_lo, u_hi, preferred_element_type=F32)) + bias_ref[...]
    log_a = _log_sigmoid_pair(x)[0] / GLA_TAU
    r_i = lax.broadcasted_iota(jnp.int32, (c, c), 0)
    c_i = lax.broadcasted_iota(jnp.int32, (c, c), 1)
    tril = (c_i <= r_i).astype(BF16)
    g_hi, g_lo = _split_bf16(log_a)
    b_all = jnp.dot(tril, g_hi, preferred_element_type=F32) + jnp.dot(tril, g_lo, preferred_element_type=F32)

    sub = GLA_SUB
    lane = lax.broadcasted_iota(jnp.int32, (sub, LANE), 1)
    jrow = lax.broadcasted_iota(jnp.int32, (sub, LANE), 0)
    gain = gain_ref[...]

    for h in range(GLA_HEADS):
        b = b_all[:, h * GLA_DK:(h + 1) * GLA_DK]
        q = q_ref[:, h * GLA_DK:(h + 1) * GLA_DK] * (GLA_DK ** -0.5)
        k = k_ref[:, h * GLA_DK:(h + 1) * GLA_DK]
        v = v_ref[:, h * GLA_DV:(h + 1) * GLA_DV].astype(BF16)
        st = st_ref[h]
        o_inter = _dot_nt((q * jnp.exp(b)).astype(BF16), st.astype(BF16))

        o_rows = []
        for blk in range(c // sub):
            lo_r, hi_r = blk * sub, (blk + 1) * sub
            b_blk, q_blk, k_blk = b[lo_r:hi_r], q[lo_r:hi_r], k[lo_r:hi_r]
            o_blk = o_inter[lo_r:hi_r]
            if blk > 0:
                b_first = b[lo_r:lo_r + 1]
                qt = (q_blk * jnp.exp(b_blk - b_first)).astype(BF16)
                kt = (k[:lo_r] * jnp.exp(b_first - b[:lo_r])).astype(BF16)
                a_off = _dot_nt(qt, kt)
                o_blk = o_blk + jnp.dot(a_off.astype(BF16), v[:lo_r], preferred_element_type=F32)
            a_t = jnp.zeros((sub, LANE), F32)
            for i in range(sub):
                e = jnp.exp(jnp.minimum(b_blk[i:i + 1] - b_blk, 0.0))
                col = jnp.sum(q_blk[i:i + 1] * k_blk * e, axis=-1, keepdims=True)
                a_t = jnp.where(lane == i, col, a_t)
            a_t = jnp.where(jrow <= lane, a_t, 0.0)
            o_diag = _dot_tn(a_t.astype(BF16), v[lo_r:hi_r])
            o_rows.append(o_blk + o_diag[:sub])
        o = jnp.concatenate(o_rows, axis=0)

        b_last = b[c - 1:c]
        kd = (k * jnp.exp(b_last - b)).astype(BF16)
        st_ref[h] = st * jnp.exp(b_last) + _dot_tn(v, kd)

        o = o * lax.rsqrt(jnp.mean(o * o, axis=-1, keepdims=True) + EPS) * gain
        g = g_ref[:, h * GLA_DV:(h + 1) * GLA_DV]
        o = o * (g / (1.0 + jnp.exp(-g)))
        o_ref[:, h * GLA_DV:(h + 1) * GLA_DV] = o.astype(o_ref.dtype)


def gla(p_gla, p_small, gate_up, gate_bias, head_gain, batch, seq):
    c = GLA_CHUNK
    n = seq // c
    wq, wv = GLA_HEADS * GLA_DK, GLA_HEADS * GLA_DV
    return pl.pallas_call(
        _gla_kernel,
        grid=(batch, n),
        in_specs=[pl.BlockSpec((c, wq), lambda b, i: (b * n + i, 0)),
                  pl.BlockSpec((c, wq), lambda b, i: (b * n + i, 1)),
                  pl.BlockSpec((c, wv), lambda b, i: (b * n + i, 1)),
                  pl.BlockSpec((c, wv), lambda b, i: (b * n + i, 2)),
                  pl.BlockSpec((c, LANE), lambda b, i: (b * n + i, 0)),
                  pl.BlockSpec((GLA_RANK, wq), lambda b, i: (0, 0)),
                  pl.BlockSpec((1, wq), lambda b, i: (0, 0)),
                  pl.BlockSpec((1, GLA_DV), lambda b, i: (0, 0))],
        out_specs=pl.BlockSpec((c, wv), lambda b, i: (b * n + i, 0)),
        out_shape=jax.ShapeDtypeStruct((batch * seq, wv), BF16),
        scratch_shapes=[pltpu.VMEM((GLA_HEADS, GLA_DV, GLA_DK), F32)],
        compiler_params=_cparams(("parallel", "arbitrary")),
        name="gla",
    )(p_gla, p_gla, p_gla, p_gla, p_small, gate_up, gate_bias.reshape(1, wq), head_gain.reshape(1, GLA_DV))


def _bias_kernel(bucket_ref, rel_ref, o_ref):
    for h in range(DSA_HEADS):
        for off in range(3):
            bk = bucket_ref[off]

            def body(n, acc, bk=bk, h=h):
                return jnp.where(bk == n, rel_ref[n, h], acc)

            o_ref[h, off] = lax.fori_loop(0, N_BUCKETS, body, jnp.zeros(bk.shape, F32))


def _rel_bucket(dist):
    max_exact = N_BUCKETS // 2
    d = jnp.maximum(dist, 1).astype(F32)
    large = max_exact + (jnp.log(d / max_exact) / math.log(MAX_DISTANCE / max_exact)
                         * (N_BUCKETS - max_exact)).astype(jnp.int32)
    large = jnp.minimum(large, N_BUCKETS - 1)
    return jnp.where(dist < max_exact, dist, large)


def bias_tiles(rel_bias):
    s = jnp.arange(Q_BLOCK, dtype=jnp.int32)[:, None]
    t = jnp.arange(Q_BLOCK, dtype=jnp.int32)[None, :]
    dist = jnp.stack([off * Q_BLOCK + t - s for off in range(3)])
    bucket = _rel_bucket(jnp.maximum(dist, 0))
    return pl.pallas_call(
        _bias_kernel,
        in_specs=[pl.BlockSpec(memory_space=pltpu.VMEM), pl.BlockSpec(memory_space=pltpu.SMEM)],
        out_specs=pl.BlockSpec(memory_space=pltpu.VMEM),
        out_shape=jax.ShapeDtypeStruct((DSA_HEADS, 3, Q_BLOCK, Q_BLOCK), F32),
        name="bias_tiles",
    )(bucket, rel_bias)


DSA_CHUNK = 4 * Q_BLOCK
NEG_BIG = -1e30


def _dsa_kernel(row_ref, smq_ref, sma_ref, dk_ref, dv_ref, bias_ref, o_ref,
                key_ref, neg_ref, q2_ref, acc_ref, *, topk):
    qb, ck = Q_BLOCK, DSA_CHUNK
    i = pl.program_id(1)
    bpc = ck // qb
    nck = (i + bpc) // bpc
    s_loc = lax.broadcasted_iota(jnp.int32, (ck, qb), 0)
    t_loc = lax.broadcasted_iota(jnp.int32, (ck, qb), 1)
    lane_k = lax.broadcasted_iota(jnp.int32, (1, LANE), 1)

    def visible(r0):
        return (r0 + s_loc) <= (i * qb + t_loc)

    w_t = smq_ref[...].T * ((IDX_HEADS ** -0.5) * (IDX_DIM ** -0.5))
    for pr in range(IDX_HEADS // 2):
        qpair = row_ref[:, DSA_IQ + pr * LANE:DSA_IQ + (pr + 1) * LANE]
        zero = jnp.zeros_like(qpair)
        q2_ref[pr, :qb, :] = jnp.where(lane_k < IDX_DIM, qpair, zero)
        q2_ref[pr, qb:, :] = jnp.where(lane_k >= IDX_DIM, qpair, zero)

    def score_chunk(c, carry):
        r0 = pl.multiple_of(c * ck, ck)
        sm = sma_ref[pl.ds(r0, ck), :]
        k_dup = jnp.where(lane_k < IDX_DIM, pltpu.roll(sm, LANE - IK_LO, 1),
                          pltpu.roll(sm, IDX_DIM - IK_LO, 1)).astype(BF16)
        acc = jnp.zeros((ck, qb), F32)
        for pr in range(IDX_HEADS // 2):
            s2 = jnp.maximum(_dot_nt(k_dup, q2_ref[pr]), 0.0)
            he, ho = IW_LO + 2 * pr, IW_LO + 2 * pr + 1
            acc = acc + w_t[he:he + 1] * s2[:, :qb] + w_t[ho:ho + 1] * s2[:, qb:]
        val = jnp.where(visible(r0), acc + 0.0, -jnp.inf)
        bits = pltpu.bitcast(val, jnp.int32)
        key_ref[pl.ds(r0, ck), :] = jnp.where(bits < 0, bits ^ jnp.int32(0x7FFFFFFF), bits)
        return carry

    lax.fori_loop(0, nck, score_chunk, 0)

    kf = float(topk)

    def count(pred):
        def body(c, acc):
            r0 = pl.multiple_of(c * ck, ck)
            hit = jnp.where(pred(key_ref[pl.ds(r0, ck), :]), 1.0, 0.0)
            return acc + jnp.sum(hit.reshape(ck // 64, 64, qb), axis=0)
        acc = lax.fori_loop(0, nck, body, jnp.zeros((64, qb), F32))
        return jnp.sum(acc, axis=0, keepdims=True)

    int_min = jnp.int32(-2 ** 31)
    m0 = jnp.where(count(lambda kk: kk >= 0) >= kf, jnp.int32(0), int_min)

    def search(n, m):
        cand = m | lax.shift_left(jnp.int32(1), jnp.int32(30) - n)
        return jnp.where(count(lambda kk: kk >= cand) >= kf, cand, m)

    kth = lax.fori_loop(0, 31, search, m0)
    need = kf - count(lambda kk: kk > kth)

    tri_r = lax.broadcasted_iota(jnp.int32, (ck, ck), 0)
    tri_c = lax.broadcasted_iota(jnp.int32, (ck, ck), 1)
    tri = (tri_c <= tri_r).astype(BF16)

    def select_chunk(c, seen):
        r0 = pl.multiple_of(c * ck, ck)
        kblk = key_ref[pl.ds(r0, ck), :]
        eq = jnp.where(kblk == kth, 1.0, 0.0)
        rank = jnp.dot(tri, eq.astype(BF16), preferred_element_type=F32) + seen
        keep = jnp.where(kblk > kth, 1.0, jnp.where(rank <= need, eq, 0.0))
        keep = jnp.where(visible(r0), keep, 0.0)
        neg_ref[pl.ds(r0, ck), :] = jnp.where(keep > 0.0, 0.0, -jnp.inf)
        return seen + jnp.sum(eq, axis=0, keepdims=True)

    lax.fori_loop(0, nck, select_chunk, jnp.zeros((1, qb), F32))

    acc_ref[...] = jnp.zeros_like(acc_ref)
    scale = DSA_DH ** -0.5

    def attend_chunk(c, ml):
        r0 = pl.multiple_of(c * ck, ck)
        kc = dk_ref[pl.ds(r0, ck), :]
        vc = dv_ref[pl.ds(r0, ck), :]
        neg = neg_ref[pl.ds(r0, ck), :]
        offs = [jnp.clip(i - bpc * c - b, 0, 2) for b in range(bpc)]
        new_ml = []
        for h in range(DSA_HEADS):
            m_old, l_old = ml[h]
            bias = jnp.concatenate([bias_ref[h, off] for off in offs], axis=0)
            qh = row_ref[:, DSA_Q + h * DSA_DH:DSA_Q + (h + 1) * DSA_DH]
            lg = _dot_nt(kc, qh) * scale + bias + neg
            m_new = jnp.maximum(m_old, jnp.max(lg, axis=0, keepdims=True))
            alpha = jnp.exp(m_old - m_new)
            pexp = jnp.exp(lg - m_new)
            acc_ref[h] = alpha * acc_ref[h] + _dot_tn(vc, pexp.astype(BF16))
            new_ml.append((m_new, alpha * l_old + jnp.sum(pexp, axis=0, keepdims=True)))
        return tuple(new_ml)

    ml0 = tuple((jnp.full((1, qb), NEG_BIG, F32), jnp.zeros((1, qb), F32)) for _ in range(DSA_HEADS))
    ml = lax.fori_loop(0, nck, attend_chunk, ml0)
    for h in range(DSA_HEADS):
        o_t = acc_ref[h] / ml[h][1]
        o_ref[:, h * DSA_DH:(h + 1) * DSA_DH] = o_t.T.astype(o_ref.dtype)


def dsa(p_dsa, p_small, btiles, batch, seq):
    qb = Q_BLOCK
    nq = seq // qb
    assert seq % DSA_CHUNK == 0 and DSA_DH == LANE
    topk = min(TOPK_MAX, seq // 4)
    wq = DSA_HEADS * DSA_DH
    return pl.pallas_call(
        functools.partial(_dsa_kernel, topk=topk),
        grid=(batch, nq),
        in_specs=[pl.BlockSpec((qb, DSA_W), lambda b, i: (b * nq + i, 0)),
                  pl.BlockSpec((qb, LANE), lambda b, i: (b * nq + i, 1)),
                  pl.BlockSpec((seq, LANE), lambda b, i: (b, 1)),
                  pl.BlockSpec((seq, DSA_DH), lambda b, i: (b, DSA_K // DSA_DH)),
                  pl.BlockSpec((seq, DSA_DH), lambda b, i: (b, DSA_V // DSA_DH)),
                  pl.BlockSpec((DSA_HEADS, 3, qb, qb), lambda b, i: (0, 0, 0, 0))],
        out_specs=pl.BlockSpec((qb, wq), lambda b, i: (b * nq + i, 0)),
        out_shape=jax.ShapeDtypeStruct((batch * seq, wq), BF16),
        scratch_shapes=[pltpu.VMEM((seq, qb), jnp.int32), pltpu.VMEM((seq, qb), F32),
                        pltpu.VMEM((IDX_HEADS // 2, 2 * qb, LANE), BF16),
                        pltpu.VMEM((DSA_HEADS, DSA_DH, qb), F32)],
        compiler_params=_cparams(("parallel", "arbitrary")),
        name="dsa",
    )(p_dsa, p_small, p_small, p_dsa, p_dsa, btiles)


def _merge_kernel(oa_ref, ob_ref, oc_ref, ga_ref, gb_ref, gc_ref, wa_ref, wb_ref, wc_ref, o_ref):
    acc = None
    for o_r, g_r, w_r in ((oa_ref, ga_ref, wa_ref), (ob_ref, gb_ref, wb_ref), (oc_ref, gc_ref, wc_ref)):
        gate = 1.0 / (1.0 + jnp.exp(-g_r[...]))
        term = gate * jnp.dot(o_r[...], w_r[...], preferred_element_type=F32)
        acc = term if acc is None else acc + term
    o_ref[...] = acc.astype(o_ref.dtype)


def gated_merge(o_a, o_b, o_c, p_gates, w_branch, tm=1024, tn=512):
    m = o_a.shape[0]
    nj = D_MODEL // tn
    o_spec = pl.BlockSpec((tm, BRANCH_WIDTH), lambda i, j: (i, 0))

    def g_spec(br):
        return pl.BlockSpec((tm, tn), lambda i, j: (i, br * nj + j))

    def w_spec(br):
        return pl.BlockSpec((BRANCH_WIDTH, tn), lambda i, j: (br, j))

    return pl.pallas_call(
        _merge_kernel,
        grid=(m // tm, nj),
        in_specs=[o_spec, o_spec, o_spec, g_spec(0), g_spec(1), g_spec(2), w_spec(0), w_spec(1), w_spec(2)],
        out_specs=pl.BlockSpec((tm, tn), lambda i, j: (i, j)),
        out_shape=jax.ShapeDtypeStruct((m, D_MODEL), BF16),
        compiler_params=_cparams(("parallel", "parallel")),
        name="gated_merge",
    )(o_a, o_b, o_c, p_gates, p_gates, p_gates, w_branch, w_branch, w_branch)


def mixer(h, x, l, btiles, norm_mix_post, norm_mlp_pre, w_in_t, gla_gate_up, gla_gate_bias, gla_head_gain,
          w_branch, w_out, batch, seq):
    def proj(src, out_dtype, name, tn=WS_TN, side=None):
        off, width = src
        return matmul_ws(h, w_in_t, l, out_dtype, WS_TM, tn, width, w_row=lambda j: off + j * tn,
                         side=side, name=name)

    p_gla = proj(SRC_GLA, F32, "proj_gla")
    p_dsa = proj(SRC_DSA, BF16, "proj_dsa", tn=WS_TN // 2)
    p_sb = proj(SRC_SB, BF16, "proj_sb")
    p_gates, wb = proj(SRC_GATES, F32, "proj_gates",
                       side=w_branch.reshape(DEPTH, N_BRANCHES * BRANCH_WIDTH, D_MODEL))
    p_small = matmul_ws(h, w_in_t, l, F32, WS_TM, LANE, 2 * LANE,
                        w_row=lambda j: SMALL_ROWS[0] + j * (SMALL_ROWS[1] - SMALL_ROWS[0]), name="proj_small")
    o_a = gla(p_gla, p_small, gla_gate_up[l], gla_gate_bias[l], gla_head_gain[l], batch, seq)
    o_b = dsa(p_dsa, p_small, btiles, batch, seq)
    o_c = stickbreaking(p_sb, batch, seq)
    merged = gated_merge(o_a, o_b, o_c, p_gates, wb)
    y = matmul_ws(merged, w_out, l, F32, WS_TM, WS_TN, D_MODEL, name="out_proj")
    return post_norm_residual(y, x, norm_mix_post[l], norm_mlp_pre[l])


def kernel(x, rel_bias, norm_mix_pre, norm_mix_post, norm_mlp_pre, norm_mlp_post, w_in, gla_gate_up,
           gla_gate_bias, gla_head_gain, w_branch, w_out, w_mlp_up, w_mlp_down):
    batch, seq, d = x.shape
    xf = x.reshape(batch * seq, d)
    btiles = bias_tiles(rel_bias)
    h = rmsnorm_cast(xf, norm_mix_pre[0])
    w_in_t = jnp.swapaxes(w_in, 1, 2)
    for l in range(DEPTH):
        xf, h = mixer(h, xf, l, btiles, norm_mix_post, norm_mlp_pre, w_in_t, gla_gate_up, gla_gate_bias,
                      gla_head_gain, w_branch, w_out, batch, seq)
        u, w_down = matmul_ws(h, w_mlp_up, l, BF16, WS_TM, WS_TN, D_FF, sq_relu=True, side=w_mlp_down,
                              name="mlp_up")
        y = matmul(u, w_down, F32, tm=1024, tn=1024, tk=4096, name="mlp_down")
        g_next = norm_mix_pre[l + 1] if l + 1 < DEPTH else None
        xf, h = post_norm_residual(y, xf, norm_mlp_post[l], g_next)
    return xf.reshape(batch, seq, d)
```

```python
import functools
import math

import jax
import jax.numpy as jnp
from jax import lax
from jax.experimental import pallas as pl
from jax.experimental.pallas import tpu as pltpu

F32 = jnp.float32
BF16 = jnp.bfloat16

D_MODEL = 4096
DEPTH = 2
N_BRANCHES = 3
BRANCH_WIDTH = 1024
GLA_HEADS, GLA_DK, GLA_DV, GLA_RANK, GLA_TAU, GLA_CHUNK = 4, 128, 256, 16, 16.0, 64
GLA_SUB = 16
DSA_HEADS, DSA_DH, IDX_HEADS, IDX_DIM, TOPK_MAX = 8, 128, 32, 64, 256
SB_HEADS, SB_DH = 8, 128
Q_BLOCK = 128
N_BUCKETS, MAX_DISTANCE = 32, 128
D_FF = 4 * D_MODEL
EPS = 1e-6
LANE = 128

WS_TM, WS_TN = 1024, 1024
DOWN_TM, DOWN_TN, DOWN_TK = 1024, 1024, 4096
MERGE_TM, MERGE_TN = 1024, 512
NORM_TM = 256
SB_TQ, SB_HEADS_PER_STEP = 256, 8
SRC_GLA = (0, 3072)
SRC_GA = 3072
SRC_DSA = (3088, 3584)
SRC_IK = 6416
SRC_SB = (6512, 3072)
SRC_GATES = (9584, 3 * D_MODEL)
IN_COLS = SRC_GATES[0] + SRC_GATES[1]
DSA_Q, DSA_K, DSA_V, DSA_IQ, DSA_W = 0, 1024, 1152, 1280, 3328
SMALL_ROWS = (SRC_GA, SRC_IK - SRC_IK % LANE)
GA_LO, IK_LO, IW_LO = 0, SRC_IK % LANE, (SRC_IK + IDX_DIM) % LANE

VMEM_LIMIT = 60 * 1024 * 1024


def _cparams(sem):
    return pltpu.CompilerParams(dimension_semantics=sem, vmem_limit_bytes=VMEM_LIMIT)


def _rms_kernel(x_ref, g_ref, o_ref):
    x = x_ref[...]
    y = x * lax.rsqrt(jnp.mean(x * x, axis=-1, keepdims=True) + EPS)
    o_ref[...] = (y * g_ref[...]).astype(o_ref.dtype)


def rmsnorm_cast(x, g, tm=NORM_TM):
    m, d = x.shape
    return pl.pallas_call(
        _rms_kernel,
        grid=(m // tm,),
        in_specs=[pl.BlockSpec((tm, d), lambda i: (i, 0)), pl.BlockSpec((1, d), lambda i: (0, 0))],
        out_specs=pl.BlockSpec((tm, d), lambda i: (i, 0)),
        out_shape=jax.ShapeDtypeStruct((m, d), BF16),
        compiler_params=_cparams(("parallel",)),
        name="rmsnorm_cast",
    )(x, g.reshape(1, d))


def _post_kernel(y_ref, x_ref, gp_ref, gn_ref, xo_ref, ho_ref):
    y = y_ref[...]
    yn = y * lax.rsqrt(jnp.mean(y * y, axis=-1, keepdims=True) + EPS) * gp_ref[...]
    xn = x_ref[...] + yn
    xo_ref[...] = xn
    hn = xn * lax.rsqrt(jnp.mean(xn * xn, axis=-1, keepdims=True) + EPS) * gn_ref[...]
    ho_ref[...] = hn.astype(ho_ref.dtype)


def _post_last_kernel(y_ref, x_ref, gp_ref, xo_ref):
    y = y_ref[...]
    yn = y * lax.rsqrt(jnp.mean(y * y, axis=-1, keepdims=True) + EPS) * gp_ref[...]
    xo_ref[...] = x_ref[...] + yn


def post_norm_residual(y, x, g_post, g_next, tm=NORM_TM):
    m, d = x.shape
    row = pl.BlockSpec((tm, d), lambda i: (i, 0))
    vec = pl.BlockSpec((1, d), lambda i: (0, 0))
    if g_next is None:
        return pl.pallas_call(
            _post_last_kernel, grid=(m // tm,), in_specs=[row, row, vec], out_specs=row,
            out_shape=jax.ShapeDtypeStruct((m, d), F32),
            compiler_params=_cparams(("parallel",)), name="post_last",
        )(y, x, g_post.reshape(1, d)), None
    return pl.pallas_call(
        _post_kernel, grid=(m // tm,), in_specs=[row, row, vec, vec], out_specs=(row, row),
        out_shape=(jax.ShapeDtypeStruct((m, d), F32), jax.ShapeDtypeStruct((m, d), BF16)),
        compiler_params=_cparams(("parallel",)), name="post_norm",
    )(y, x, g_post.reshape(1, d), g_next.reshape(1, d))


def _mm_acc_kernel(x_ref, w_ref, o_ref, acc_ref, *, nk):
    k = pl.program_id(2)

    @pl.when(k == 0)
    def _():
        acc_ref[...] = jnp.zeros_like(acc_ref)

    acc_ref[...] += jnp.dot(x_ref[...], w_ref[...], preferred_element_type=F32)

    @pl.when(k == nk - 1)
    def _():
        o_ref[...] = acc_ref[...].astype(o_ref.dtype)


def matmul_ktiled(x, w, out_dtype, tm, tn, tk, name="matmul"):
    m, kd = x.shape
    _, n = w.shape
    assert m % tm == 0 and n % tn == 0 and kd % tk == 0
    nk = kd // tk
    return pl.pallas_call(
        functools.partial(_mm_acc_kernel, nk=nk),
        grid=(m // tm, n // tn, nk),
        in_specs=[pl.BlockSpec((tm, tk), lambda i, j, k: (i, k)),
                  pl.BlockSpec((tk, tn), lambda i, j, k: (k, j))],
        out_specs=pl.BlockSpec((tm, tn), lambda i, j, k: (i, j)),
        out_shape=jax.ShapeDtypeStruct((m, n), out_dtype),
        scratch_shapes=[pltpu.VMEM((tm, tn), F32)],
        compiler_params=_cparams(("parallel", "parallel", "arbitrary")),
        name=name,
    )(x, w)


def _mm_ws_kernel(*refs, l, tn, nj, w_row, sq_relu, has_side):
    if has_side:
        x_ref, w_hbm, side_ref, o_ref, side_o_ref, stage_ref, wb_ref, sem = refs
        side_o_ref[...] = side_ref[...].astype(side_o_ref.dtype)
    else:
        x_ref, w_hbm, o_ref, stage_ref, wb_ref, sem = refs
    j, i = pl.program_id(0), pl.program_id(1)
    w_rows_are_outputs = w_row is not None

    def tile_copy(jt):
        if w_rows_are_outputs:
            src = w_hbm.at[l, pl.ds(pl.multiple_of(w_row(jt), 8), tn), :]
        else:
            src = w_hbm.at[l, :, pl.ds(pl.multiple_of(jt * tn, LANE), tn)]
        return pltpu.make_async_copy(src, stage_ref, sem)

    @pl.when((j == 0) & (i == 0))
    def _():
        tile_copy(0).start()

    @pl.when(i == 0)
    def _():
        tile_copy(j).wait()
        wb_ref[...] = stage_ref[...].astype(wb_ref.dtype)

    @pl.when((i == 1) & (j + 1 < nj))
    def _():
        tile_copy(j + 1).start()

    if w_rows_are_outputs:
        r = lax.dot_general(x_ref[...], wb_ref[...], (((1,), (1,)), ((), ())), preferred_element_type=F32)
    else:
        r = jnp.dot(x_ref[...], wb_ref[...], preferred_element_type=F32)
    if sq_relu:
        r = jnp.square(jnp.maximum(r, 0.0))
    o_ref[...] = r.astype(o_ref.dtype)


def matmul_ws(x, w, l, out_dtype, tm, tn, n_out, w_row=None, sq_relu=False, side=None, name="matmul_ws"):
    m, kd = x.shape
    nj, ni = n_out // tn, m // tm
    assert m % tm == 0 and n_out % tn == 0 and ni >= 2
    w_shape = (kd, tn) if w_row is None else (tn, kd)
    in_specs = [pl.BlockSpec((tm, kd), lambda j, i: (i, 0)), pl.BlockSpec(memory_space=pl.ANY)]
    out_specs = pl.BlockSpec((tm, tn), lambda j, i: (i, j))
    out_shape = jax.ShapeDtypeStruct((m, n_out), out_dtype)
    args = (x, w)
    if side is not None:
        _, sr, sc = side.shape
        assert sr % (nj * ni * 16) == 0
        slab = sr // (nj * ni)
        in_specs.append(pl.BlockSpec((None, slab, sc), lambda j, i: (l, j * ni + i, 0)))
        out_specs = (out_specs, pl.BlockSpec((slab, sc), lambda j, i: (j * ni + i, 0)))
        out_shape = (out_shape, jax.ShapeDtypeStruct((sr, sc), BF16))
        args = (x, w, side)
    return pl.pallas_call(
        functools.partial(_mm_ws_kernel, l=l, tn=tn, nj=nj, w_row=w_row, sq_relu=sq_relu,
                          has_side=side is not None),
        grid=(nj, ni),
        in_specs=in_specs,
        out_specs=out_specs,
        out_shape=out_shape,
        scratch_shapes=[pltpu.VMEM(w_shape, F32), pltpu.VMEM(w_shape, BF16), pltpu.SemaphoreType.DMA(())],
        compiler_params=_cparams(("arbitrary", "arbitrary")),
        name=name,
    )(*args)


def _split_bf16(x):
    hi = x.astype(BF16)
    lo = (x - hi.astype(F32)).astype(BF16)
    return hi, lo


LOG2E = 1.4426950408889634


def _log2_sigmoid_pair(z2):
    lo, hi = jnp.minimum(z2, 0.0), jnp.maximum(z2, 0.0)
    neg_sp = jnp.log(1.0 + jnp.exp2(lo - hi)) * (-LOG2E)
    return lo + neg_sp, neg_sp - hi


def _dot_nt(a, b):
    return lax.dot_general(a, b, (((1,), (1,)), ((), ())), preferred_element_type=F32)


def _dot_tn(a, b):
    return lax.dot_general(a, b, (((0,), (0,)), ((), ())), preferred_element_type=F32)


def _sb_kernel(q_ref, k_ref, v_ref, o_ref, *, tq, nh):
    qi = pl.program_id(2)
    dh = SB_DH
    scale2 = dh ** -0.5 * LOG2E
    nsub = tq // LANE
    rr = lax.broadcasted_iota(jnp.int32, (LANE, 2 * LANE), 0)
    cc = lax.broadcasted_iota(jnp.int32, (LANE, 2 * LANE), 1)
    cum_rhs = ((cc >= LANE) | (rr > cc)).astype(BF16)
    cum_rhs = jnp.concatenate([cum_rhs, cum_rhs], axis=0)
    rows = lax.broadcasted_iota(jnp.int32, (tq, tq), 0)
    cols = lax.broadcasted_iota(jnp.int32, (tq, tq), 1)
    strict = cols < rows

    def tile(j, state, diag):
        start = pl.multiple_of(j * tq, tq)
        new_state = []
        for h in range(nh):
            run, acc = state[h]
            hs = slice(h * dh, (h + 1) * dh)
            z2 = _dot_nt(q_ref[:, hs], k_ref[pl.ds(start, tq), hs]) * scale2
            log_beta, log_1m = _log2_sigmoid_pair(z2)
            if diag:
                log_1m = jnp.where(strict, log_1m, 0.0)
            hi, lo = _split_bf16(log_1m)
            after = [None] * nsub
            for sb in reversed(range(nsub)):
                cs = slice(sb * LANE, (sb + 1) * LANE)
                r2 = jnp.dot(jnp.concatenate([hi[:, cs], lo[:, cs]], axis=1), cum_rhs,
                             preferred_element_type=F32)
                after[sb] = r2[:, :LANE] + run
                run = run + r2[:, LANE:]
            w = jnp.exp2(log_beta + jnp.concatenate(after, axis=1))
            if diag:
                w = jnp.where(strict, w, 0.0)
            acc = acc + jnp.dot(w.astype(BF16), v_ref[pl.ds(start, tq), hs], preferred_element_type=F32)
            new_state.append((run, acc))
        return tuple(new_state)

    zero = jnp.zeros((tq, LANE), F32)
    state = tile(qi, tuple((zero, zero) for _ in range(nh)), True)
    state = lax.fori_loop(1, qi + 1, lambda jj, s: tile(qi - jj, s, False), state)
    for h in range(nh):
        o_ref[:, h * dh:(h + 1) * dh] = state[h][1].astype(o_ref.dtype)


def stickbreaking(p_sb, batch, seq, tq=SB_TQ, nh=SB_HEADS_PER_STEP):
    assert SB_DH == LANE
    nq = seq // tq
    ng = SB_HEADS // nh
    wb = nh * SB_DH
    return pl.pallas_call(
        functools.partial(_sb_kernel, tq=tq, nh=nh),
        grid=(batch, ng, nq),
        in_specs=[pl.BlockSpec((tq, wb), lambda b, g, i: (b * nq + i, g)),
                  pl.BlockSpec((seq, wb), lambda b, g, i: (b, ng + g)),
                  pl.BlockSpec((seq, wb), lambda b, g, i: (b, 2 * ng + g))],
        out_specs=pl.BlockSpec((tq, wb), lambda b, g, i: (b * nq + i, g)),
        out_shape=jax.ShapeDtypeStruct((batch * seq, SB_HEADS * SB_DH), BF16),
        compiler_params=_cparams(("parallel", "parallel", "arbitrary")),
        name="stickbreaking",
    )(p_sb, p_sb, p_sb)


def _gla_kernel(q_ref, k_ref, v_ref, g_ref, small_ref, up_ref, bias_ref, gain_ref, o_ref, st_ref):
    c = GLA_CHUNK

    @pl.when(pl.program_id(1) == 0)
    def _():
        st_ref[...] = jnp.zeros_like(st_ref)

    a_hi, a_lo = _split_bf16(small_ref[:, GA_LO:GA_LO + GLA_RANK])
    u_hi, u_lo = _split_bf16(up_ref[...])
    x = (jnp.dot(a_hi, u_hi, preferred_element_type=F32) + jnp.dot(a_hi, u_lo, preferred_element_type=F32)
         + jnp.dot(a_lo, u_hi, preferred_element_type=F32)) + bias_ref[...]
    log_a = _log2_sigmoid_pair(x * LOG2E)[0] / GLA_TAU
    r_i = lax.broadcasted_iota(jnp.int32, (c, c), 0)
    c_i = lax.broadcasted_iota(jnp.int32, (c, c), 1)
    tril = (c_i <= r_i).astype(BF16)
    g_hi, g_lo = _split_bf16(log_a)
    b_all = jnp.dot(tril, g_hi, preferred_element_type=F32) + jnp.dot(tril, g_lo, preferred_element_type=F32)

    sub = GLA_SUB
    lane = lax.broadcasted_iota(jnp.int32, (sub, LANE), 1)
    jrow = lax.broadcasted_iota(jnp.int32, (sub, LANE), 0)
    gain = gain_ref[...]

    for h in range(GLA_HEADS):
        b = b_all[:, h * GLA_DK:(h + 1) * GLA_DK]
        q = q_ref[:, h * GLA_DK:(h + 1) * GLA_DK] * (GLA_DK ** -0.5)
        k = k_ref[:, h * GLA_DK:(h + 1) * GLA_DK]
        v = v_ref[:, h * GLA_DV:(h + 1) * GLA_DV].astype(BF16)
        st = st_ref[h]
        o_inter = _dot_nt((q * jnp.exp2(b)).astype(BF16), st.astype(BF16))

        o_rows = []
        for blk in range(c // sub):
            lo_r, hi_r = blk * sub, (blk + 1) * sub
            b_blk, q_blk, k_blk = b[lo_r:hi_r], q[lo_r:hi_r], k[lo_r:hi_r]
            o_blk = o_inter[lo_r:hi_r]
            if blk > 0:
                b_first = b[lo_r:lo_r + 1]
                qt = (q_blk * jnp.exp2(b_blk - b_first)).astype(BF16)
                kt = (k[:lo_r] * jnp.exp2(b_first - b[:lo_r])).astype(BF16)
                a_off = _dot_nt(qt, kt)
                o_blk = o_blk + jnp.dot(a_off.astype(BF16), v[:lo_r], preferred_element_type=F32)
            a_t = jnp.zeros((sub, LANE), F32)
            for i in range(sub):
                e = jnp.exp2(jnp.minimum(b_blk[i:i + 1] - b_blk, 0.0))
                col = jnp.sum(q_blk[i:i + 1] * k_blk * e, axis=-1, keepdims=True)
                a_t = jnp.where(lane == i, col, a_t)
            a_t = jnp.where(jrow <= lane, a_t, 0.0)
            o_diag = _dot_tn(a_t.astype(BF16), v[lo_r:hi_r])
            o_rows.append(o_blk + o_diag[:sub])
        o = jnp.concatenate(o_rows, axis=0)

        b_last = b[c - 1:c]
        kd = (k * jnp.exp2(b_last - b)).astype(BF16)
        st_ref[h] = st * jnp.exp2(b_last) + _dot_tn(v, kd)

        o = o * lax.rsqrt(jnp.mean(o * o, axis=-1, keepdims=True) + EPS) * gain
        g = g_ref[:, h * GLA_DV:(h + 1) * GLA_DV]
        o = o * (g / (1.0 + jnp.exp(-g)))
        o_ref[:, h * GLA_DV:(h + 1) * GLA_DV] = o.astype(o_ref.dtype)


def gla(p_gla, p_small, gate_up, gate_bias, head_gain, batch, seq):
    c = GLA_CHUNK
    n = seq // c
    wq, wv = GLA_HEADS * GLA_DK, GLA_HEADS * GLA_DV
    return pl.pallas_call(
        _gla_kernel,
        grid=(batch, n),
        in_specs=[pl.BlockSpec((c, wq), lambda b, i: (b * n + i, 0)),
                  pl.BlockSpec((c, wq), lambda b, i: (b * n + i, 1)),
                  pl.BlockSpec((c, wv), lambda b, i: (b * n + i, 1)),
                  pl.BlockSpec((c, wv), lambda b, i: (b * n + i, 2)),
                  pl.BlockSpec((c, LANE), lambda b, i: (b * n + i, 0)),
                  pl.BlockSpec((GLA_RANK, wq), lambda b, i: (0, 0)),
                  pl.BlockSpec((1, wq), lambda b, i: (0, 0)),
                  pl.BlockSpec((1, GLA_DV), lambda b, i: (0, 0))],
        out_specs=pl.BlockSpec((c, wv), lambda b, i: (b * n + i, 0)),
        out_shape=jax.ShapeDtypeStruct((batch * seq, wv), BF16),
        scratch_shapes=[pltpu.VMEM((GLA_HEADS, GLA_DV, GLA_DK), F32)],
        compiler_params=_cparams(("parallel", "arbitrary")),
        name="gla",
    )(p_gla, p_gla, p_gla, p_gla, p_small, gate_up, gate_bias.reshape(1, wq), head_gain.reshape(1, GLA_DV))


def _bias_kernel(bucket_ref, rel_ref, o_ref):
    for h in range(DSA_HEADS):
        for off in range(3):
            bk = bucket_ref[off]

            def body(n, acc, bk=bk, h=h):
                return jnp.where(bk == n, rel_ref[n, h], acc)

            o_ref[h, off] = lax.fori_loop(0, N_BUCKETS, body, jnp.zeros(bk.shape, F32)) * LOG2E


def _rel_bucket(dist):
    max_exact = N_BUCKETS // 2
    d = jnp.maximum(dist, 1).astype(F32)
    large = max_exact + (jnp.log(d / max_exact) / math.log(MAX_DISTANCE / max_exact)
                         * (N_BUCKETS - max_exact)).astype(jnp.int32)
    large = jnp.minimum(large, N_BUCKETS - 1)
    return jnp.where(dist < max_exact, dist, large)


def bias_tiles(rel_bias):
    s = jnp.arange(Q_BLOCK, dtype=jnp.int32)[:, None]
    t = jnp.arange(Q_BLOCK, dtype=jnp.int32)[None, :]
    dist = jnp.stack([off * Q_BLOCK + t - s for off in range(3)])
    bucket = _rel_bucket(jnp.maximum(dist, 0))
    return pl.pallas_call(
        _bias_kernel,
        in_specs=[pl.BlockSpec(memory_space=pltpu.VMEM), pl.BlockSpec(memory_space=pltpu.SMEM)],
        out_specs=pl.BlockSpec(memory_space=pltpu.VMEM),
        out_shape=jax.ShapeDtypeStruct((DSA_HEADS, 3, Q_BLOCK, Q_BLOCK), F32),
        name="bias_tiles",
    )(bucket, rel_bias)


DSA_CHUNK = 4 * Q_BLOCK
NEG_BIG = -1e30


def _dsa_kernel(row_ref, smq_ref, sma_ref, dk_ref, dv_ref, bias_ref, o_ref,
                key_ref, neg_ref, q2_ref, acc_ref, *, topk):
    qb, ck = Q_BLOCK, DSA_CHUNK
    i = pl.program_id(1)
    bpc = ck // qb
    nck = (i + bpc) // bpc
    s_loc = lax.broadcasted_iota(jnp.int32, (ck, qb), 0)
    t_loc = lax.broadcasted_iota(jnp.int32, (ck, qb), 1)
    lane_k = lax.broadcasted_iota(jnp.int32, (1, LANE), 1)

    def visible(r0):
        return (r0 + s_loc) <= (i * qb + t_loc)

    w_t = smq_ref[...].T * ((IDX_HEADS ** -0.5) * (IDX_DIM ** -0.5))
    for pr in range(IDX_HEADS // 2):
        qpair = row_ref[:, DSA_IQ + pr * LANE:DSA_IQ + (pr + 1) * LANE]
        zero = jnp.zeros_like(qpair)
        q2_ref[pr, :qb, :] = jnp.where(lane_k < IDX_DIM, qpair, zero)
        q2_ref[pr, qb:, :] = jnp.where(lane_k >= IDX_DIM, qpair, zero)

    def score_chunk(c, carry):
        r0 = pl.multiple_of(c * ck, ck)
        sm = sma_ref[pl.ds(r0, ck), :]
        k_dup = jnp.where(lane_k < IDX_DIM, pltpu.roll(sm, LANE - IK_LO, 1),
                          pltpu.roll(sm, IDX_DIM - IK_LO, 1)).astype(BF16)
        acc = jnp.zeros((ck, qb), F32)
        for pr in range(IDX_HEADS // 2):
            s2 = jnp.maximum(_dot_nt(k_dup, q2_ref[pr]), 0.0)
            he, ho = IW_LO + 2 * pr, IW_LO + 2 * pr + 1
            acc = acc + w_t[he:he + 1] * s2[:, :qb] + w_t[ho:ho + 1] * s2[:, qb:]
        val = jnp.where(visible(r0), acc + 0.0, -jnp.inf)
        bits = pltpu.bitcast(val, jnp.int32)
        key_ref[pl.ds(r0, ck), :] = jnp.where(bits < 0, bits ^ jnp.int32(0x7FFFFFFF), bits)
        return carry

    lax.fori_loop(0, nck, score_chunk, 0)

    kf = float(topk)

    def count(pred):
        def body(c, acc):
            r0 = pl.multiple_of(c * ck, ck)
            hit = jnp.where(pred(key_ref[pl.ds(r0, ck), :]), 1.0, 0.0)
            return acc + jnp.sum(hit.reshape(ck // 64, 64, qb), axis=0)
        acc = lax.fori_loop(0, nck, body, jnp.zeros((64, qb), F32))
        return jnp.sum(acc, axis=0, keepdims=True)

    int_min = jnp.int32(-2 ** 31)
    m0 = jnp.where(count(lambda kk: kk >= 0) >= kf, jnp.int32(0), int_min)

    def search(n, m):
        cand = m | lax.shift_left(jnp.int32(1), jnp.int32(30) - n)
        return jnp.where(count(lambda kk: kk >= cand) >= kf, cand, m)

    kth = lax.fori_loop(0, 31, search, m0)
    need = kf - count(lambda kk: kk > kth)

    tri_r = lax.broadcasted_iota(jnp.int32, (ck, ck), 0)
    tri_c = lax.broadcasted_iota(jnp.int32, (ck, ck), 1)
    tri = (tri_c <= tri_r).astype(BF16)

    def select_chunk(c, seen):
        r0 = pl.multiple_of(c * ck, ck)
        kblk = key_ref[pl.ds(r0, ck), :]
        eq = jnp.where(kblk == kth, 1.0, 0.0)
        rank = jnp.dot(tri, eq.astype(BF16), preferred_element_type=F32) + seen
        keep = jnp.where(kblk > kth, 1.0, jnp.where(rank <= need, eq, 0.0))
        keep = jnp.where(visible(r0), keep, 0.0)
        neg_ref[pl.ds(r0, ck), :] = jnp.where(keep > 0.0, 0.0, -jnp.inf)
        return seen + jnp.sum(eq, axis=0, keepdims=True)

    lax.fori_loop(0, nck, select_chunk, jnp.zeros((1, qb), F32))

    acc_ref[...] = jnp.zeros_like(acc_ref)
    scale2 = DSA_DH ** -0.5 * LOG2E

    def attend_chunk(c, ml):
        r0 = pl.multiple_of(c * ck, ck)
        kc = dk_ref[pl.ds(r0, ck), :]
        vc = dv_ref[pl.ds(r0, ck), :]
        neg = neg_ref[pl.ds(r0, ck), :]
        offs = [jnp.clip(i - bpc * c - b, 0, 2) for b in range(bpc)]
        new_ml = []
        for h in range(DSA_HEADS):
            m_old, l_old = ml[h]
            bias = jnp.concatenate([bias_ref[h, off] for off in offs], axis=0)
            qh = row_ref[:, DSA_Q + h * DSA_DH:DSA_Q + (h + 1) * DSA_DH]
            lg = _dot_nt(kc, qh) * scale2 + bias + neg
            m_new = jnp.maximum(m_old, jnp.max(lg, axis=0, keepdims=True))
            alpha = jnp.exp2(m_old - m_new)
            pexp = jnp.exp2(lg - m_new)
            acc_ref[h] = alpha * acc_ref[h] + _dot_tn(vc, pexp.astype(BF16))
            new_ml.append((m_new, alpha * l_old + jnp.sum(pexp, axis=0, keepdims=True)))
        return tuple(new_ml)

    ml0 = tuple((jnp.full((1, qb), NEG_BIG, F32), jnp.zeros((1, qb), F32)) for _ in range(DSA_HEADS))
    ml = lax.fori_loop(0, nck, attend_chunk, ml0)
    for h in range(DSA_HEADS):
        o_t = acc_ref[h] / ml[h][1]
        o_ref[:, h * DSA_DH:(h + 1) * DSA_DH] = o_t.T.astype(o_ref.dtype)


def dsa(p_dsa, p_small, btiles, batch, seq):
    qb = Q_BLOCK
    nq = seq // qb
    assert seq % DSA_CHUNK == 0 and DSA_DH == LANE
    topk = min(TOPK_MAX, seq // 4)
    wq = DSA_HEADS * DSA_DH
    return pl.pallas_call(
        functools.partial(_dsa_kernel, topk=topk),
        grid=(batch, nq),
        in_specs=[pl.BlockSpec((qb, DSA_W), lambda b, i: (b * nq + i, 0)),
                  pl.BlockSpec((qb, LANE), lambda b, i: (b * nq + i, 1)),
                  pl.BlockSpec((seq, LANE), lambda b, i: (b, 1)),
                  pl.BlockSpec((seq, DSA_DH), lambda b, i: (b, DSA_K // DSA_DH)),
                  pl.BlockSpec((seq, DSA_DH), lambda b, i: (b, DSA_V // DSA_DH)),
                  pl.BlockSpec((DSA_HEADS, 3, qb, qb), lambda b, i: (0, 0, 0, 0))],
        out_specs=pl.BlockSpec((qb, wq), lambda b, i: (b * nq + i, 0)),
        out_shape=jax.ShapeDtypeStruct((batch * seq, wq), BF16),
        scratch_shapes=[pltpu.VMEM((seq, qb), jnp.int32), pltpu.VMEM((seq, qb), F32),
                        pltpu.VMEM((IDX_HEADS // 2, 2 * qb, LANE), BF16),
                        pltpu.VMEM((DSA_HEADS, DSA_DH, qb), F32)],
        compiler_params=_cparams(("parallel", "arbitrary")),
        name="dsa",
    )(p_dsa, p_small, p_small, p_dsa, p_dsa, btiles)


def _merge_kernel(oa_ref, ob_ref, oc_ref, ga_ref, gb_ref, gc_ref, wa_ref, wb_ref, wc_ref, o_ref):
    acc = None
    for o_r, g_r, w_r in ((oa_ref, ga_ref, wa_ref), (ob_ref, gb_ref, wb_ref), (oc_ref, gc_ref, wc_ref)):
        gate = 1.0 / (1.0 + jnp.exp(-g_r[...]))
        term = gate * jnp.dot(o_r[...], w_r[...], preferred_element_type=F32)
        acc = term if acc is None else acc + term
    o_ref[...] = acc.astype(o_ref.dtype)


def gated_merge(o_a, o_b, o_c, p_gates, w_branch, tm=MERGE_TM, tn=MERGE_TN):
    m = o_a.shape[0]
    nj = D_MODEL // tn
    o_spec = pl.BlockSpec((tm, BRANCH_WIDTH), lambda i, j: (i, 0))

    def g_spec(br):
        return pl.BlockSpec((tm, tn), lambda i, j: (i, br * nj + j))

    def w_spec(br):
        return pl.BlockSpec((BRANCH_WIDTH, tn), lambda i, j: (br, j))

    return pl.pallas_call(
        _merge_kernel,
        grid=(m // tm, nj),
        in_specs=[o_spec, o_spec, o_spec, g_spec(0), g_spec(1), g_spec(2), w_spec(0), w_spec(1), w_spec(2)],
        out_specs=pl.BlockSpec((tm, tn), lambda i, j: (i, j)),
        out_shape=jax.ShapeDtypeStruct((m, D_MODEL), BF16),
        compiler_params=_cparams(("parallel", "parallel")),
        name="gated_merge",
    )(o_a, o_b, o_c, p_gates, p_gates, p_gates, w_branch, w_branch, w_branch)


def mixer(h, x, l, btiles, norm_mix_post, norm_mlp_pre, w_in_t, gla_gate_up, gla_gate_bias, gla_head_gain,
          w_branch, w_out, batch, seq):
    def proj(src, out_dtype, name, tn=WS_TN, side=None):
        off, width = src
        return matmul_ws(h, w_in_t, l, out_dtype, WS_TM, tn, width, w_row=lambda j: off + j * tn,
                         side=side, name=name)

    p_gla = proj(SRC_GLA, F32, "proj_gla")
    p_dsa = proj(SRC_DSA, BF16, "proj_dsa", tn=WS_TN // 2)
    p_sb = proj(SRC_SB, BF16, "proj_sb")
    p_gates, wb = proj(SRC_GATES, F32, "proj_gates",
                       side=w_branch.reshape(DEPTH, N_BRANCHES * BRANCH_WIDTH, D_MODEL))
    p_small = matmul_ws(h, w_in_t, l, F32, WS_TM, LANE, 2 * LANE,
                        w_row=lambda j: SMALL_ROWS[0] + j * (SMALL_ROWS[1] - SMALL_ROWS[0]), name="proj_small")
    o_a = gla(p_gla, p_small, gla_gate_up[l], gla_gate_bias[l], gla_head_gain[l], batch, seq)
    o_b = dsa(p_dsa, p_small, btiles, batch, seq)
    o_c = stickbreaking(p_sb, batch, seq)
    merged = gated_merge(o_a, o_b, o_c, p_gates, wb)
    y = matmul_ws(merged, w_out, l, F32, WS_TM, WS_TN, D_MODEL, name="out_proj")
    return post_norm_residual(y, x, norm_mix_post[l], norm_mlp_pre[l])


def kernel(x, rel_bias, norm_mix_pre, norm_mix_post, norm_mlp_pre, norm_mlp_post, w_in, gla_gate_up,
           gla_gate_bias, gla_head_gain, w_branch, w_out, w_mlp_up, w_mlp_down):
    batch, seq, d = x.shape
    xf = x.reshape(batch * seq, d)
    btiles = bias_tiles(rel_bias)
    h = rmsnorm_cast(xf, norm_mix_pre[0])
    w_in_t = jnp.swapaxes(w_in, 1, 2)
    for l in range(DEPTH):
        xf, h = mixer(h, xf, l, btiles, norm_mix_post, norm_mlp_pre, w_in_t, gla_gate_up, gla_gate_bias,
                      gla_head_gain, w_branch, w_out, batch, seq)
        u, w_down = matmul_ws(h, w_mlp_up, l, BF16, WS_TM, WS_TN, D_FF, sq_relu=True, side=w_mlp_down,
                              name="mlp_up")
        y = matmul_ktiled(u, w_down, F32, DOWN_TM, DOWN_TN, DOWN_TK, name="mlp_down")
        g_next = norm_mix_pre[l + 1] if l + 1 < DEPTH else None
        xf, h = post_norm_residual(y, xf, norm_mlp_post[l], g_next)
    return xf.reshape(batch, seq, d)
```

```python
import functools
import math
from typing import Callable, NamedTuple

import jax
import jax.numpy as jnp
from jax import lax
from jax.experimental import pallas as pl
from jax.experimental.pallas import tpu as pltpu

F32 = jnp.float32
BF16 = jnp.bfloat16

D_MODEL = 4096
DEPTH = 2
N_BRANCHES = 3
BRANCH_WIDTH = 1024
GLA_HEADS, GLA_DK, GLA_DV, GLA_RANK, GLA_TAU, GLA_CHUNK = 4, 128, 256, 16, 16.0, 64
GLA_SUB = 16
DSA_HEADS, DSA_DH, IDX_HEADS, IDX_DIM, TOPK_MAX = 8, 128, 32, 64, 256
SB_HEADS, SB_DH = 8, 128
Q_BLOCK = 128
N_BUCKETS, MAX_DISTANCE = 32, 128
D_FF = 4 * D_MODEL
EPS = 1e-6
LANE = 128

WS_TM, WS_TN = 1024, 1024
GATES_TN = 512
DOWN_TM, DOWN_TN, DOWN_TK = 1024, 1024, 4096
MERGE_TM, MERGE_TN = 1024, 512
NORM_TM = 256
SB_TQ = 256
SRC_GLA = (0, 3072)
SRC_GA = 3072
SRC_DSA = (3088, 3584)
SRC_IK = 6416
SRC_SB = (6512, 3072)
SRC_GATES = (9584, 3 * D_MODEL)
IN_COLS = SRC_GATES[0] + SRC_GATES[1]
DSA_Q, DSA_K, DSA_V, DSA_IQ, DSA_W = 0, 1024, 1152, 1280, 3328
SMALL_ROWS = (SRC_GA, SRC_IK - SRC_IK % LANE)
GA_LO, IK_LO, IW_LO = 0, SRC_IK % LANE, (SRC_IK + IDX_DIM) % LANE

VMEM_LIMIT = 60 * 1024 * 1024


def _cparams(sem):
    return pltpu.CompilerParams(dimension_semantics=sem, vmem_limit_bytes=VMEM_LIMIT)


def _rms_kernel(x_ref, g_ref, o_ref):
    x = x_ref[...]
    y = x * lax.rsqrt(jnp.mean(x * x, axis=-1, keepdims=True) + EPS)
    o_ref[...] = (y * g_ref[...]).astype(o_ref.dtype)


def rmsnorm_cast(x, g, tm=NORM_TM):
    m, d = x.shape
    return pl.pallas_call(
        _rms_kernel,
        grid=(m // tm,),
        in_specs=[pl.BlockSpec((tm, d), lambda i: (i, 0)), pl.BlockSpec((1, d), lambda i: (0, 0))],
        out_specs=pl.BlockSpec((tm, d), lambda i: (i, 0)),
        out_shape=jax.ShapeDtypeStruct((m, d), BF16),
        compiler_params=_cparams(("parallel",)),
        name="rmsnorm_cast",
    )(x, g.reshape(1, d))


def _post_kernel(y_ref, x_ref, gp_ref, gn_ref, xo_ref, ho_ref):
    y = y_ref[...]
    yn = y * lax.rsqrt(jnp.mean(y * y, axis=-1, keepdims=True) + EPS) * gp_ref[...]
    xn = x_ref[...] + yn
    xo_ref[...] = xn
    hn = xn * lax.rsqrt(jnp.mean(xn * xn, axis=-1, keepdims=True) + EPS) * gn_ref[...]
    ho_ref[...] = hn.astype(ho_ref.dtype)


def _post_last_kernel(y_ref, x_ref, gp_ref, xo_ref):
    y = y_ref[...]
    yn = y * lax.rsqrt(jnp.mean(y * y, axis=-1, keepdims=True) + EPS) * gp_ref[...]
    xo_ref[...] = x_ref[...] + yn


def post_norm_residual(y, x, g_post, g_next, tm=NORM_TM):
    m, d = x.shape
    row = pl.BlockSpec((tm, d), lambda i: (i, 0))
    vec = pl.BlockSpec((1, d), lambda i: (0, 0))
    if g_next is None:
        return pl.pallas_call(
            _post_last_kernel, grid=(m // tm,), in_specs=[row, row, vec], out_specs=row,
            out_shape=jax.ShapeDtypeStruct((m, d), F32),
            compiler_params=_cparams(("parallel",)), name="post_last",
        )(y, x, g_post.reshape(1, d)), None
    return pl.pallas_call(
        _post_kernel, grid=(m // tm,), in_specs=[row, row, vec, vec], out_specs=(row, row),
        out_shape=(jax.ShapeDtypeStruct((m, d), F32), jax.ShapeDtypeStruct((m, d), BF16)),
        compiler_params=_cparams(("parallel",)), name="post_norm",
    )(y, x, g_post.reshape(1, d), g_next.reshape(1, d))


def _mm_acc_kernel(x_ref, w_ref, o_ref, acc_ref, *, nk):
    k = pl.program_id(2)

    @pl.when(k == 0)
    def _():
        acc_ref[...] = jnp.zeros_like(acc_ref)

    acc_ref[...] += jnp.dot(x_ref[...], w_ref[...], preferred_element_type=F32)

    @pl.when(k == nk - 1)
    def _():
        o_ref[...] = acc_ref[...].astype(o_ref.dtype)


def matmul_ktiled(x, w, out_dtype, tm, tn, tk, name="matmul"):
    m, kd = x.shape
    _, n = w.shape
    assert m % tm == 0 and n % tn == 0 and kd % tk == 0
    nk = kd // tk
    return pl.pallas_call(
        functools.partial(_mm_acc_kernel, nk=nk),
        grid=(m // tm, n // tn, nk),
        in_specs=[pl.BlockSpec((tm, tk), lambda i, j, k: (i, k)),
                  pl.BlockSpec((tk, tn), lambda i, j, k: (k, j))],
        out_specs=pl.BlockSpec((tm, tn), lambda i, j, k: (i, j)),
        out_shape=jax.ShapeDtypeStruct((m, n), out_dtype),
        scratch_shapes=[pltpu.VMEM((tm, tn), F32)],
        compiler_params=_cparams(("parallel", "parallel", "arbitrary")),
        name=name,
    )(x, w)


def _mm_ws_kernel(*refs, l, tn, nj, w_row, sq_relu, has_side, rider):
    refs = list(refs)
    n_tab, n_rin, n_rout = (len(rider.tables), len(rider.in_specs), len(rider.out_specs)) if rider else (0, 0, 0)
    tables = [refs.pop(0) for _ in range(n_tab)]
    x_ref, w_hbm = refs.pop(0), refs.pop(0)
    side_ref = refs.pop(0) if has_side else None
    rider_ins = [refs.pop(0) for _ in range(n_rin)]
    o_ref = refs.pop(0)
    side_o_ref = refs.pop(0) if has_side else None
    rider_outs = [refs.pop(0) for _ in range(n_rout)]
    stage_ref, wb_ref, sem = refs.pop(0), refs.pop(0), refs.pop(0)
    rider_scratch = refs
    j, i = pl.program_id(0), pl.program_id(1)
    w_rows_are_outputs = w_row is not None
    if has_side:
        side_o_ref[...] = side_ref[...].astype(side_o_ref.dtype)

    def tile_copy(jt):
        if w_rows_are_outputs:
            src = w_hbm.at[l, pl.ds(pl.multiple_of(w_row(jt), 8), tn), :]
        else:
            src = w_hbm.at[l, :, pl.ds(pl.multiple_of(jt * tn, LANE), tn)]
        return pltpu.make_async_copy(src, stage_ref, sem)

    @pl.when((j == 0) & (i == 0))
    def _():
        tile_copy(0).start()

    @pl.when(i == 0)
    def _():
        tile_copy(j).wait()
        wb_ref[...] = stage_ref[...].astype(wb_ref.dtype)

    @pl.when((i == 1) & (j + 1 < nj))
    def _():
        tile_copy(j + 1).start()

    def product(ks):
        if w_rows_are_outputs:
            return lax.dot_general(x_ref[:, ks], wb_ref[:, ks], (((1,), (1,)), ((), ())),
                                   preferred_element_type=F32)
        return jnp.dot(x_ref[:, ks], wb_ref[ks, :], preferred_element_type=F32)

    if rider:
        assert o_ref.dtype == F32 and not sq_relu
        kd = x_ref.shape[1]
        slab = kd // rider.n_fill
        done = [0]

        def fill():
            s = done[0]
            done[0] += 1
            part = product(slice(s * slab, (s + 1) * slab))
            if s == 0:
                o_ref[...] = part
            else:
                o_ref[...] += part

        step = j * pl.num_programs(1) + i
        has_work = tables[-1][step] != 0

        @pl.when(has_work)
        def _():
            rider.body(step, tables, rider_ins, rider_outs, rider_scratch, fill)
            assert done[0] == rider.n_fill

        @pl.when(jnp.logical_not(has_work))
        def _():
            o_ref[...] = product(slice(None))
    else:
        r = product(slice(None))
        if sq_relu:
            r = jnp.square(jnp.maximum(r, 0.0))
        o_ref[...] = r.astype(o_ref.dtype)


class Rider(NamedTuple):
    tables: tuple
    args: tuple
    in_specs: tuple
    out_specs: tuple
    out_shape: tuple
    scratch_shapes: tuple
    n_fill: int
    body: Callable


def matmul_ws(x, w, l, out_dtype, tm, tn, n_out, w_row=None, sq_relu=False, side=None, rider=None,
              name="matmul_ws"):
    m, kd = x.shape
    nj, ni = n_out // tn, m // tm
    assert m % tm == 0 and n_out % tn == 0 and ni >= 2
    w_shape = (kd, tn) if w_row is None else (tn, kd)
    in_specs = [pl.BlockSpec((tm, kd), lambda j, i, *_: (i, 0)), pl.BlockSpec(memory_space=pl.ANY)]
    out_specs = [pl.BlockSpec((tm, tn), lambda j, i, *_: (i, j))]
    out_shape = [jax.ShapeDtypeStruct((m, n_out), out_dtype)]
    args = [x, w]
    scratch = [pltpu.VMEM(w_shape, F32), pltpu.VMEM(w_shape, BF16), pltpu.SemaphoreType.DMA(())]
    if side is not None:
        _, sr, sc = side.shape
        assert sr % (nj * ni * 16) == 0
        slab = sr // (nj * ni)
        in_specs.append(pl.BlockSpec((None, slab, sc), lambda j, i, *_: (l, j * ni + i, 0)))
        out_specs.append(pl.BlockSpec((slab, sc), lambda j, i, *_: (j * ni + i, 0)))
        out_shape.append(jax.ShapeDtypeStruct((sr, sc), BF16))
        args.append(side)
    tables = ()
    if rider is not None:
        tables = rider.tables
        in_specs += list(rider.in_specs)
        out_specs += list(rider.out_specs)
        out_shape += list(rider.out_shape)
        args += list(rider.args)
        scratch += list(rider.scratch_shapes)
    out = pl.pallas_call(
        functools.partial(_mm_ws_kernel, l=l, tn=tn, nj=nj, w_row=w_row, sq_relu=sq_relu,
                          has_side=side is not None, rider=rider),
        grid_spec=pltpu.PrefetchScalarGridSpec(
            num_scalar_prefetch=len(tables), grid=(nj, ni), in_specs=in_specs, out_specs=out_specs,
            scratch_shapes=scratch),
        out_shape=out_shape,
        compiler_params=_cparams(("arbitrary", "arbitrary")),
        name=name,
    )(*tables, *args)
    return out[0] if len(out) == 1 else tuple(out)


def _split_bf16(x):
    hi = x.astype(BF16)
    lo = (x - hi.astype(F32)).astype(BF16)
    return hi, lo


LOG2E = 1.4426950408889634


def _log2_sigmoid_pair(z2):
    lo, hi = jnp.minimum(z2, 0.0), jnp.maximum(z2, 0.0)
    neg_sp = jnp.log(1.0 + jnp.exp2(lo - hi)) * (-LOG2E)
    return lo + neg_sp, neg_sp - hi


def _dot_nt(a, b):
    return lax.dot_general(a, b, (((1,), (1,)), ((), ())), preferred_element_type=F32)


def _dot_tn(a, b):
    return lax.dot_general(a, b, (((0,), (0,)), ((), ())), preferred_element_type=F32)


def _sb_pair_body(step, tables, ins, outs, scratch, fill, *, tq):
    _, q_tab, k_tab, _ = tables
    q_ref, k_ref, v_ref = ins
    (o_ref,) = outs
    run_ref, acc_ref = scratch
    qi, kj = q_tab[step], k_tab[step]

    @pl.when(step == 0)
    def _():
        run_ref[...] = jnp.zeros_like(run_ref)
        acc_ref[...] = jnp.zeros_like(acc_ref)

    dh = SB_DH
    scale2 = dh ** -0.5 * LOG2E
    nsub = tq // LANE
    rr = lax.broadcasted_iota(jnp.int32, (LANE, 2 * LANE), 0)
    cc = lax.broadcasted_iota(jnp.int32, (LANE, 2 * LANE), 1)
    cum_rhs = ((cc >= LANE) | (rr > cc)).astype(BF16)
    cum_rhs = jnp.concatenate([cum_rhs, cum_rhs], axis=0)
    rows = lax.broadcasted_iota(jnp.int32, (tq, tq), 0)
    cols = lax.broadcasted_iota(jnp.int32, (tq, tq), 1)
    strict = (cols - rows) < (qi - kj) * tq
    first = kj == qi
    for h in range(SB_HEADS):
        hs = slice(h * dh, (h + 1) * dh)
        run = jnp.where(first, 0.0, run_ref[h])
        acc = jnp.where(first, 0.0, acc_ref[h])
        z2 = _dot_nt(q_ref[:, hs], k_ref[:, hs]) * scale2
        fill()
        log_beta, log_1m = _log2_sigmoid_pair(z2)
        log_1m = jnp.where(strict, log_1m, 0.0)
        hi, lo = _split_bf16(log_1m)
        after = [None] * nsub
        for sb in reversed(range(nsub)):
            cs = slice(sb * LANE, (sb + 1) * LANE)
            r2 = jnp.dot(jnp.concatenate([hi[:, cs], lo[:, cs]], axis=1), cum_rhs, preferred_element_type=F32)
            after[sb] = r2[:, :LANE] + run
            run = run + r2[:, LANE:]
        fill()
        w = jnp.where(strict, jnp.exp2(log_beta + jnp.concatenate(after, axis=1)), 0.0)
        acc = acc + jnp.dot(w.astype(BF16), v_ref[:, hs], preferred_element_type=F32)
        run_ref[h] = run
        acc_ref[h] = acc
        o_ref[:, hs] = acc.astype(o_ref.dtype)


def stickbreaking_rider(p_sb, batch, seq, nj, ni, tq=SB_TQ):
    assert SB_DH == LANE
    nq = seq // tq
    items = [(b, qi, kj, 1) for b in range(batch) for qi in range(nq) for kj in range(qi, -1, -1)]
    assert len(items) <= nj * ni
    items += [items[-1][:3] + (0,)] * (nj * ni - len(items))
    tables = tuple(jnp.asarray([it[c] for it in items], jnp.int32) for c in range(4))
    wd = SB_HEADS * SB_DH

    def spec(tab, col):
        return pl.BlockSpec((tq, wd), lambda j, i, bt, qt, kt, work: (
            bt[j * ni + i] * nq + (qt, kt)[tab][j * ni + i], col))

    return Rider(
        tables=tables, args=(p_sb, p_sb, p_sb),
        in_specs=(spec(0, 0), spec(1, 1), spec(1, 2)),
        out_specs=(spec(0, 0),),
        out_shape=(jax.ShapeDtypeStruct((batch * seq, wd), BF16),),
        scratch_shapes=(pltpu.VMEM((SB_HEADS, tq, LANE), F32), pltpu.VMEM((SB_HEADS, tq, LANE), F32)),
        n_fill=2 * SB_HEADS, body=functools.partial(_sb_pair_body, tq=tq))


def _gla_kernel(q_ref, k_ref, v_ref, g_ref, small_ref, up_ref, bias_ref, gain_ref, o_ref, st_ref):
    c = GLA_CHUNK

    @pl.when(pl.program_id(1) == 0)
    def _():
        st_ref[...] = jnp.zeros_like(st_ref)

    a_hi, a_lo = _split_bf16(small_ref[:, GA_LO:GA_LO + GLA_RANK])
    u_hi, u_lo = _split_bf16(up_ref[...])
    x = (jnp.dot(a_hi, u_hi, preferred_element_type=F32) + jnp.dot(a_hi, u_lo, preferred_element_type=F32)
         + jnp.dot(a_lo, u_hi, preferred_element_type=F32)) + bias_ref[...]
    log_a = _log2_sigmoid_pair(x * LOG2E)[0] / GLA_TAU
    r_i = lax.broadcasted_iota(jnp.int32, (c, c), 0)
    c_i = lax.broadcasted_iota(jnp.int32, (c, c), 1)
    tril = (c_i <= r_i).astype(BF16)
    g_hi, g_lo = _split_bf16(log_a)
    b_all = jnp.dot(tril, g_hi, preferred_element_type=F32) + jnp.dot(tril, g_lo, preferred_element_type=F32)

    sub = GLA_SUB
    lane = lax.broadcasted_iota(jnp.int32, (sub, LANE), 1)
    jrow = lax.broadcasted_iota(jnp.int32, (sub, LANE), 0)
    gain = gain_ref[...]

    for h in range(GLA_HEADS):
        b = b_all[:, h * GLA_DK:(h + 1) * GLA_DK]
        q = q_ref[:, h * GLA_DK:(h + 1) * GLA_DK] * (GLA_DK ** -0.5)
        k = k_ref[:, h * GLA_DK:(h + 1) * GLA_DK]
        v = v_ref[:, h * GLA_DV:(h + 1) * GLA_DV].astype(BF16)
        st = st_ref[h]
        o_inter = _dot_nt((q * jnp.exp2(b)).astype(BF16), st.astype(BF16))

        o_rows = []
        for blk in range(c // sub):
            lo_r, hi_r = blk * sub, (blk + 1) * sub
            b_blk, q_blk, k_blk = b[lo_r:hi_r], q[lo_r:hi_r], k[lo_r:hi_r]
            o_blk = o_inter[lo_r:hi_r]
            if blk > 0:
                b_first = b[lo_r:lo_r + 1]
                qt = (q_blk * jnp.exp2(b_blk - b_first)).astype(BF16)
                kt = (k[:lo_r] * jnp.exp2(b_first - b[:lo_r])).astype(BF16)
                a_off = _dot_nt(qt, kt)
                o_blk = o_blk + jnp.dot(a_off.astype(BF16), v[:lo_r], preferred_element_type=F32)
            a_t = jnp.zeros((sub, LANE), F32)
            for i in range(sub):
                e = jnp.exp2(jnp.minimum(b_blk[i:i + 1] - b_blk, 0.0))
                col = jnp.sum(q_blk[i:i + 1] * k_blk * e, axis=-1, keepdims=True)
                a_t = jnp.where(lane == i, col, a_t)
            a_t = jnp.where(jrow <= lane, a_t, 0.0)
            o_diag = _dot_tn(a_t.astype(BF16), v[lo_r:hi_r])
            o_rows.append(o_blk + o_diag[:sub])
        o = jnp.concatenate(o_rows, axis=0)

        b_last = b[c - 1:c]
        kd = (k * jnp.exp2(b_last - b)).astype(BF16)
        st_ref[h] = st * jnp.exp2(b_last) + _dot_tn(v, kd)

        o = o * lax.rsqrt(jnp.mean(o * o, axis=-1, keepdims=True) + EPS) * gain
        g = g_ref[:, h * GLA_DV:(h + 1) * GLA_DV]
        o = o * (g / (1.0 + jnp.exp(-g)))
        o_ref[:, h * GLA_DV:(h + 1) * GLA_DV] = o.astype(o_ref.dtype)


def gla(p_gla, p_small, gate_up, gate_bias, head_gain, batch, seq):
    c = GLA_CHUNK
    n = seq // c
    wq, wv = GLA_HEADS * GLA_DK, GLA_HEADS * GLA_DV
    return pl.pallas_call(
        _gla_kernel,
        grid=(batch, n),
        in_specs=[pl.BlockSpec((c, wq), lambda b, i: (b * n + i, 0)),
                  pl.BlockSpec((c, wq), lambda b, i: (b * n + i, 1)),
                  pl.BlockSpec((c, wv), lambda b, i: (b * n + i, 1)),
                  pl.BlockSpec((c, wv), lambda b, i: (b * n + i, 2)),
                  pl.BlockSpec((c, LANE), lambda b, i: (b * n + i, 0)),
                  pl.BlockSpec((GLA_RANK, wq), lambda b, i: (0, 0)),
                  pl.BlockSpec((1, wq), lambda b, i: (0, 0)),
                  pl.BlockSpec((1, GLA_DV), lambda b, i: (0, 0))],
        out_specs=pl.BlockSpec((c, wv), lambda b, i: (b * n + i, 0)),
        out_shape=jax.ShapeDtypeStruct((batch * seq, wv), BF16),
        scratch_shapes=[pltpu.VMEM((GLA_HEADS, GLA_DV, GLA_DK), F32)],
        compiler_params=_cparams(("parallel", "arbitrary")),
        name="gla",
    )(p_gla, p_gla, p_gla, p_gla, p_small, gate_up, gate_bias.reshape(1, wq), head_gain.reshape(1, GLA_DV))


def _bias_kernel(bucket_ref, rel_ref, o_ref):
    for h in range(DSA_HEADS):
        for off in range(3):
            bk = bucket_ref[off]

            def body(n, acc, bk=bk, h=h):
                return jnp.where(bk == n, rel_ref[n, h], acc)

            o_ref[h, off] = lax.fori_loop(0, N_BUCKETS, body, jnp.zeros(bk.shape, F32)) * LOG2E


def _rel_bucket(dist):
    max_exact = N_BUCKETS // 2
    d = jnp.maximum(dist, 1).astype(F32)
    large = max_exact + (jnp.log(d / max_exact) / math.log(MAX_DISTANCE / max_exact)
                         * (N_BUCKETS - max_exact)).astype(jnp.int32)
    large = jnp.minimum(large, N_BUCKETS - 1)
    return jnp.where(dist < max_exact, dist, large)


def bias_tiles(rel_bias):
    s = jnp.arange(Q_BLOCK, dtype=jnp.int32)[:, None]
    t = jnp.arange(Q_BLOCK, dtype=jnp.int32)[None, :]
    dist = jnp.stack([off * Q_BLOCK + t - s for off in range(3)])
    bucket = _rel_bucket(jnp.maximum(dist, 0))
    return pl.pallas_call(
        _bias_kernel,
        in_specs=[pl.BlockSpec(memory_space=pltpu.VMEM), pl.BlockSpec(memory_space=pltpu.SMEM)],
        out_specs=pl.BlockSpec(memory_space=pltpu.VMEM),
        out_shape=jax.ShapeDtypeStruct((DSA_HEADS, 3, Q_BLOCK, Q_BLOCK), F32),
        name="bias_tiles",
    )(bucket, rel_bias)


DSA_CHUNK = 4 * Q_BLOCK
NEG_BIG = -1e30


def _dsa_kernel(row_ref, smq_ref, sma_ref, dk_ref, dv_ref, bias_ref, o_ref,
                key_ref, neg_ref, q2_ref, acc_ref, *, topk):
    qb, ck = Q_BLOCK, DSA_CHUNK
    i = pl.program_id(1)
    bpc = ck // qb
    nck = (i + bpc) // bpc
    s_loc = lax.broadcasted_iota(jnp.int32, (ck, qb), 0)
    t_loc = lax.broadcasted_iota(jnp.int32, (ck, qb), 1)
    lane_k = lax.broadcasted_iota(jnp.int32, (1, LANE), 1)

    def visible(r0):
        return (r0 + s_loc) <= (i * qb + t_loc)

    w_t = smq_ref[...].T * ((IDX_HEADS ** -0.5) * (IDX_DIM ** -0.5))
    for pr in range(IDX_HEADS // 2):
        qpair = row_ref[:, DSA_IQ + pr * LANE:DSA_IQ + (pr + 1) * LANE]
        zero = jnp.zeros_like(qpair)
        q2_ref[pr, :qb, :] = jnp.where(lane_k < IDX_DIM, qpair, zero)
        q2_ref[pr, qb:, :] = jnp.where(lane_k >= IDX_DIM, qpair, zero)

    def score_chunk(c, carry):
        r0 = pl.multiple_of(c * ck, ck)
        sm = sma_ref[pl.ds(r0, ck), :]
        k_dup = jnp.where(lane_k < IDX_DIM, pltpu.roll(sm, LANE - IK_LO, 1),
                          pltpu.roll(sm, IDX_DIM - IK_LO, 1)).astype(BF16)
        acc = jnp.zeros((ck, qb), F32)
        for pr in range(IDX_HEADS // 2):
            s2 = jnp.maximum(_dot_nt(k_dup, q2_ref[pr]), 0.0)
            he, ho = IW_LO + 2 * pr, IW_LO + 2 * pr + 1
            acc = acc + w_t[he:he + 1] * s2[:, :qb] + w_t[ho:ho + 1] * s2[:, qb:]
        val = jnp.where(visible(r0), acc + 0.0, -jnp.inf)
        bits = pltpu.bitcast(val, jnp.int32)
        key_ref[pl.ds(r0, ck), :] = jnp.where(bits < 0, bits ^ jnp.int32(0x7FFFFFFF), bits)
        return carry

    lax.fori_loop(0, nck, score_chunk, 0)

    kf = float(topk)

    def count(pred):
        def body(c, acc):
            r0 = pl.multiple_of(c * ck, ck)
            hit = jnp.where(pred(key_ref[pl.ds(r0, ck), :]), 1.0, 0.0)
            return acc + jnp.sum(hit.reshape(ck // 64, 64, qb), axis=0)
        acc = lax.fori_loop(0, nck, body, jnp.zeros((64, qb), F32))
        return jnp.sum(acc, axis=0, keepdims=True)

    int_min = jnp.int32(-2 ** 31)
    m0 = jnp.where(count(lambda kk: kk >= 0) >= kf, jnp.int32(0), int_min)

    def search(n, m):
        cand = m | lax.shift_left(jnp.int32(1), jnp.int32(30) - n)
        return jnp.where(count(lambda kk: kk >= cand) >= kf, cand, m)

    kth = lax.fori_loop(0, 31, search, m0)
    need = kf - count(lambda kk: kk > kth)

    tri_r = lax.broadcasted_iota(jnp.int32, (ck, ck), 0)
    tri_c = lax.broadcasted_iota(jnp.int32, (ck, ck), 1)
    tri = (tri_c <= tri_r).astype(BF16)

    def select_chunk(c, seen):
        r0 = pl.multiple_of(c * ck, ck)
        kblk = key_ref[pl.ds(r0, ck), :]
        eq = jnp.where(kblk == kth, 1.0, 0.0)
        rank = jnp.dot(tri, eq.astype(BF16), preferred_element_type=F32) + seen
        keep = jnp.where(kblk > kth, 1.0, jnp.where(rank <= need, eq, 0.0))
        keep = jnp.where(visible(r0), keep, 0.0)
        neg_ref[pl.ds(r0, ck), :] = jnp.where(keep > 0.0, 0.0, -jnp.inf)
        return seen + jnp.sum(eq, axis=0, keepdims=True)

    lax.fori_loop(0, nck, select_chunk, jnp.zeros((1, qb), F32))

    acc_ref[...] = jnp.zeros_like(acc_ref)
    scale2 = DSA_DH ** -0.5 * LOG2E

    def attend_chunk(c, ml):
        r0 = pl.multiple_of(c * ck, ck)
        kc = dk_ref[pl.ds(r0, ck), :]
        vc = dv_ref[pl.ds(r0, ck), :]
        neg = neg_ref[pl.ds(r0, ck), :]
        offs = [jnp.clip(i - bpc * c - b, 0, 2) for b in range(bpc)]
        new_ml = []
        for h in range(DSA_HEADS):
            m_old, l_old = ml[h]
            bias = jnp.concatenate([bias_ref[h, off] for off in offs], axis=0)
            qh = row_ref[:, DSA_Q + h * DSA_DH:DSA_Q + (h + 1) * DSA_DH]
            lg = _dot_nt(kc, qh) * scale2 + bias + neg
            m_new = jnp.maximum(m_old, jnp.max(lg, axis=0, keepdims=True))
            alpha = jnp.exp2(m_old - m_new)
            pexp = jnp.exp2(lg - m_new)
            acc_ref[h] = alpha * acc_ref[h] + _dot_tn(vc, pexp.astype(BF16))
            new_ml.append((m_new, alpha * l_old + jnp.sum(pexp, axis=0, keepdims=True)))
        return tuple(new_ml)

    ml0 = tuple((jnp.full((1, qb), NEG_BIG, F32), jnp.zeros((1, qb), F32)) for _ in range(DSA_HEADS))
    ml = lax.fori_loop(0, nck, attend_chunk, ml0)
    for h in range(DSA_HEADS):
        o_t = acc_ref[h] / ml[h][1]
        o_ref[:, h * DSA_DH:(h + 1) * DSA_DH] = o_t.T.astype(o_ref.dtype)


def dsa(p_dsa, p_small, btiles, batch, seq):
    qb = Q_BLOCK
    nq = seq // qb
    assert seq % DSA_CHUNK == 0 and DSA_DH == LANE
    topk = min(TOPK_MAX, seq // 4)
    wq = DSA_HEADS * DSA_DH
    return pl.pallas_call(
        functools.partial(_dsa_kernel, topk=topk),
        grid=(batch, nq),
        in_specs=[pl.BlockSpec((qb, DSA_W), lambda b, i: (b * nq + i, 0)),
                  pl.BlockSpec((qb, LANE), lambda b, i: (b * nq + i, 1)),
                  pl.BlockSpec((seq, LANE), lambda b, i: (b, 1)),
                  pl.BlockSpec((seq, DSA_DH), lambda b, i: (b, DSA_K // DSA_DH)),
                  pl.BlockSpec((seq, DSA_DH), lambda b, i: (b, DSA_V // DSA_DH)),
                  pl.BlockSpec((DSA_HEADS, 3, qb, qb), lambda b, i: (0, 0, 0, 0))],
        out_specs=pl.BlockSpec((qb, wq), lambda b, i: (b * nq + i, 0)),
        out_shape=jax.ShapeDtypeStruct((batch * seq, wq), BF16),
        scratch_shapes=[pltpu.VMEM((seq, qb), jnp.int32), pltpu.VMEM((seq, qb), F32),
                        pltpu.VMEM((IDX_HEADS // 2, 2 * qb, LANE), BF16),
                        pltpu.VMEM((DSA_HEADS, DSA_DH, qb), F32)],
        compiler_params=_cparams(("parallel", "arbitrary")),
        name="dsa",
    )(p_dsa, p_small, p_small, p_dsa, p_dsa, btiles)


def _merge_kernel(oa_ref, ob_ref, oc_ref, ga_ref, gb_ref, gc_ref, wa_ref, wb_ref, wc_ref, o_ref):
    acc = None
    for o_r, g_r, w_r in ((oa_ref, ga_ref, wa_ref), (ob_ref, gb_ref, wb_ref), (oc_ref, gc_ref, wc_ref)):
        gate = 1.0 / (1.0 + jnp.exp(-g_r[...]))
        term = gate * jnp.dot(o_r[...], w_r[...], preferred_element_type=F32)
        acc = term if acc is None else acc + term
    o_ref[...] = acc.astype(o_ref.dtype)


def gated_merge(o_a, o_b, o_c, p_gates, w_branch, tm=MERGE_TM, tn=MERGE_TN):
    m = o_a.shape[0]
    nj = D_MODEL // tn
    o_spec = pl.BlockSpec((tm, BRANCH_WIDTH), lambda i, j: (i, 0))

    def g_spec(br):
        return pl.BlockSpec((tm, tn), lambda i, j: (i, br * nj + j))

    def w_spec(br):
        return pl.BlockSpec((BRANCH_WIDTH, tn), lambda i, j: (br, j))

    return pl.pallas_call(
        _merge_kernel,
        grid=(m // tm, nj),
        in_specs=[o_spec, o_spec, o_spec, g_spec(0), g_spec(1), g_spec(2), w_spec(0), w_spec(1), w_spec(2)],
        out_specs=pl.BlockSpec((tm, tn), lambda i, j: (i, j)),
        out_shape=jax.ShapeDtypeStruct((m, D_MODEL), BF16),
        compiler_params=_cparams(("parallel", "parallel")),
        name="gated_merge",
    )(o_a, o_b, o_c, p_gates, p_gates, p_gates, w_branch, w_branch, w_branch)


def mixer(h, x, l, btiles, norm_mix_post, norm_mlp_pre, w_in_t, gla_gate_up, gla_gate_bias, gla_head_gain,
          w_branch, w_out, batch, seq):
    def proj(src, out_dtype, name, tn=WS_TN, side=None, rider=None):
        off, width = src
        return matmul_ws(h, w_in_t, l, out_dtype, WS_TM, tn, width, w_row=lambda j: off + j * tn,
                         side=side, rider=rider, name=name)

    p_gla = proj(SRC_GLA, F32, "proj_gla")
    p_dsa = proj(SRC_DSA, BF16, "proj_dsa", tn=WS_TN // 2)
    p_sb = proj(SRC_SB, BF16, "proj_sb")
    sb = stickbreaking_rider(p_sb, batch, seq, SRC_GATES[1] // GATES_TN, h.shape[0] // WS_TM)
    p_gates, wb, o_c = proj(SRC_GATES, F32, "proj_gates_sb", tn=GATES_TN, rider=sb,
                            side=w_branch.reshape(DEPTH, N_BRANCHES * BRANCH_WIDTH, D_MODEL))
    p_small = matmul_ws(h, w_in_t, l, F32, WS_TM, LANE, 2 * LANE,
                        w_row=lambda j: SMALL_ROWS[0] + j * (SMALL_ROWS[1] - SMALL_ROWS[0]), name="proj_small")
    o_a = gla(p_gla, p_small, gla_gate_up[l], gla_gate_bias[l], gla_head_gain[l], batch, seq)
    o_b = dsa(p_dsa, p_small, btiles, batch, seq)
    merged = gated_merge(o_a, o_b, o_c, p_gates, wb)
    y = matmul_ws(merged, w_out, l, F32, WS_TM, WS_TN, D_MODEL, name="out_proj")
    return post_norm_residual(y, x, norm_mix_post[l], norm_mlp_pre[l])


def kernel(x, rel_bias, norm_mix_pre, norm_mix_post, norm_mlp_pre, norm_mlp_post, w_in, gla_gate_up,
           gla_gate_bias, gla_head_gain, w_branch, w_out, w_mlp_up, w_mlp_down):
    batch, seq, d = x.shape
    xf = x.reshape(batch * seq, d)
    btiles = bias_tiles(rel_bias)
    h = rmsnorm_cast(xf, norm_mix_pre[0])
    w_in_t = jnp.swapaxes(w_in, 1, 2)
    for l in range(DEPTH):
        xf, h = mixer(h, xf, l, btiles, norm_mix_post, norm_mlp_pre, w_in_t, gla_gate_up, gla_gate_bias,
                      gla_head_gain, w_branch, w_out, batch, seq)
        u, w_down = matmul_ws(h, w_mlp_up, l, BF16, WS_TM, WS_TN, D_FF, sq_relu=True, side=w_mlp_down,
                              name="mlp_up")
        y = matmul_ktiled(u, w_down, F32, DOWN_TM, DOWN_TN, DOWN_TK, name="mlp_down")
        g_next = norm_mix_pre[l + 1] if l + 1 < DEPTH else None
        xf, h = post_norm_residual(y, xf, norm_mlp_post[l], g_next)
    return xf.reshape(batch, seq, d)
```

```python
import functools
import math
from typing import Callable, NamedTuple

import jax
import jax.numpy as jnp
from jax import lax
from jax.experimental import pallas as pl
from jax.experimental.pallas import tpu as pltpu

F32 = jnp.float32
BF16 = jnp.bfloat16

D_MODEL = 4096
DEPTH = 2
N_BRANCHES = 3
BRANCH_WIDTH = 1024
GLA_HEADS, GLA_DK, GLA_DV, GLA_RANK, GLA_TAU, GLA_CHUNK = 4, 128, 256, 16, 16.0, 64
GLA_SUB = 16
DSA_HEADS, DSA_DH, IDX_HEADS, IDX_DIM, TOPK_MAX = 8, 128, 32, 64, 256
SB_HEADS, SB_DH = 8, 128
Q_BLOCK = 128
N_BUCKETS, MAX_DISTANCE = 32, 128
D_FF = 4 * D_MODEL
EPS = 1e-6
LANE = 128

WS_TM, WS_TN = 1024, 1024
GATES_TN = 512
DOWN_TM, DOWN_TN, DOWN_TK = 1024, 1024, 4096
MERGE_TM, MERGE_TN = 1024, 512
NORM_TM = 256
SB_TQ = 256
SRC_GLA = (0, 3072)
SRC_GA = 3072
SRC_DSA = (3088, 3584)
SRC_IK = 6416
SRC_SB = (6512, 3072)
SRC_GATES = (9584, 3 * D_MODEL)
IN_COLS = SRC_GATES[0] + SRC_GATES[1]
DSA_Q, DSA_K, DSA_V, DSA_IQ, DSA_W = 0, 1024, 1152, 1280, 3328
SMALL_ROWS = (SRC_GA, SRC_IK - SRC_IK % LANE)
GA_LO, IK_LO, IW_LO = 0, SRC_IK % LANE, (SRC_IK + IDX_DIM) % LANE

VMEM_LIMIT = 60 * 1024 * 1024


def _cparams(sem):
    return pltpu.CompilerParams(dimension_semantics=sem, vmem_limit_bytes=VMEM_LIMIT)


def _rms_kernel(x_ref, g_ref, o_ref):
    x = x_ref[...]
    y = x * lax.rsqrt(jnp.mean(x * x, axis=-1, keepdims=True) + EPS)
    o_ref[...] = (y * g_ref[...]).astype(o_ref.dtype)


def rmsnorm_cast(x, g, tm=NORM_TM):
    m, d = x.shape
    return pl.pallas_call(
        _rms_kernel,
        grid=(m // tm,),
        in_specs=[pl.BlockSpec((tm, d), lambda i: (i, 0)), pl.BlockSpec((1, d), lambda i: (0, 0))],
        out_specs=pl.BlockSpec((tm, d), lambda i: (i, 0)),
        out_shape=jax.ShapeDtypeStruct((m, d), BF16),
        compiler_params=_cparams(("parallel",)),
        name="rmsnorm_cast",
    )(x, g.reshape(1, d))


def _post_kernel(y_ref, x_ref, gp_ref, gn_ref, xo_ref, ho_ref):
    y = y_ref[...]
    yn = y * lax.rsqrt(jnp.mean(y * y, axis=-1, keepdims=True) + EPS) * gp_ref[...]
    xn = x_ref[...] + yn
    xo_ref[...] = xn
    hn = xn * lax.rsqrt(jnp.mean(xn * xn, axis=-1, keepdims=True) + EPS) * gn_ref[...]
    ho_ref[...] = hn.astype(ho_ref.dtype)


def _post_last_kernel(y_ref, x_ref, gp_ref, xo_ref):
    y = y_ref[...]
    yn = y * lax.rsqrt(jnp.mean(y * y, axis=-1, keepdims=True) + EPS) * gp_ref[...]
    xo_ref[...] = x_ref[...] + yn


def post_norm_residual(y, x, g_post, g_next, tm=NORM_TM):
    m, d = x.shape
    row = pl.BlockSpec((tm, d), lambda i: (i, 0))
    vec = pl.BlockSpec((1, d), lambda i: (0, 0))
    if g_next is None:
        return pl.pallas_call(
            _post_last_kernel, grid=(m // tm,), in_specs=[row, row, vec], out_specs=row,
            out_shape=jax.ShapeDtypeStruct((m, d), F32),
            compiler_params=_cparams(("parallel",)), name="post_last",
        )(y, x, g_post.reshape(1, d)), None
    return pl.pallas_call(
        _post_kernel, grid=(m // tm,), in_specs=[row, row, vec, vec], out_specs=(row, row),
        out_shape=(jax.ShapeDtypeStruct((m, d), F32), jax.ShapeDtypeStruct((m, d), BF16)),
        compiler_params=_cparams(("parallel",)), name="post_norm",
    )(y, x, g_post.reshape(1, d), g_next.reshape(1, d))


def _mm_acc_kernel(x_ref, w_ref, o_ref, acc_ref, *, nk):
    k = pl.program_id(2)

    @pl.when(k == 0)
    def _():
        acc_ref[...] = jnp.zeros_like(acc_ref)

    acc_ref[...] += jnp.dot(x_ref[...], w_ref[...], preferred_element_type=F32)

    @pl.when(k == nk - 1)
    def _():
        o_ref[...] = acc_ref[...].astype(o_ref.dtype)


def matmul_ktiled(x, w, out_dtype, tm, tn, tk, name="matmul"):
    m, kd = x.shape
    _, n = w.shape
    assert m % tm == 0 and n % tn == 0 and kd % tk == 0
    nk = kd // tk
    return pl.pallas_call(
        functools.partial(_mm_acc_kernel, nk=nk),
        grid=(m // tm, n // tn, nk),
        in_specs=[pl.BlockSpec((tm, tk), lambda i, j, k: (i, k)),
                  pl.BlockSpec((tk, tn), lambda i, j, k: (k, j))],
        out_specs=pl.BlockSpec((tm, tn), lambda i, j, k: (i, j)),
        out_shape=jax.ShapeDtypeStruct((m, n), out_dtype),
        scratch_shapes=[pltpu.VMEM((tm, tn), F32)],
        compiler_params=_cparams(("parallel", "parallel", "arbitrary")),
        name=name,
    )(x, w)


def _mm_ws_kernel(*refs, l, tn, nj, w_row, sq_relu, has_side, rider):
    refs = list(refs)
    n_tab, n_rin, n_rout = (len(rider.tables), len(rider.in_specs), len(rider.out_specs)) if rider else (0, 0, 0)
    tables = [refs.pop(0) for _ in range(n_tab)]
    x_ref, w_hbm = refs.pop(0), refs.pop(0)
    side_ref = refs.pop(0) if has_side else None
    rider_ins = [refs.pop(0) for _ in range(n_rin)]
    o_ref = refs.pop(0)
    side_o_ref = refs.pop(0) if has_side else None
    rider_outs = [refs.pop(0) for _ in range(n_rout)]
    stage_ref, wb_ref, sem = refs.pop(0), refs.pop(0), refs.pop(0)
    rider_scratch = refs
    j, i = pl.program_id(0), pl.program_id(1)
    w_rows_are_outputs = w_row is not None
    if has_side:
        side_o_ref[...] = side_ref[...].astype(side_o_ref.dtype)

    def tile_copy(jt):
        if w_rows_are_outputs:
            src = w_hbm.at[l, pl.ds(pl.multiple_of(w_row(jt), 8), tn), :]
        else:
            src = w_hbm.at[l, :, pl.ds(pl.multiple_of(jt * tn, LANE), tn)]
        return pltpu.make_async_copy(src, stage_ref, sem)

    @pl.when((j == 0) & (i == 0))
    def _():
        tile_copy(0).start()

    @pl.when(i == 0)
    def _():
        tile_copy(j).wait()
        wb_ref[...] = stage_ref[...].astype(wb_ref.dtype)

    @pl.when((i == 1) & (j + 1 < nj))
    def _():
        tile_copy(j + 1).start()

    def product(ks):
        if w_rows_are_outputs:
            return lax.dot_general(x_ref[:, ks], wb_ref[:, ks], (((1,), (1,)), ((), ())),
                                   preferred_element_type=F32)
        return jnp.dot(x_ref[:, ks], wb_ref[ks, :], preferred_element_type=F32)

    if rider:
        assert o_ref.dtype == F32 and not sq_relu
        kd = x_ref.shape[1]
        slab = kd // rider.n_fill
        done = [0]

        def fill():
            s = done[0]
            done[0] += 1
            part = product(slice(s * slab, (s + 1) * slab))
            if s == 0:
                o_ref[...] = part
            else:
                o_ref[...] += part

        step = j * pl.num_programs(1) + i
        has_work = tables[-1][step] != 0

        @pl.when(has_work)
        def _():
            rider.body(step, tables, rider_ins, rider_outs, rider_scratch, fill)
            assert done[0] == rider.n_fill

        @pl.when(jnp.logical_not(has_work))
        def _():
            o_ref[...] = product(slice(None))
    else:
        r = product(slice(None))
        if sq_relu:
            r = jnp.square(jnp.maximum(r, 0.0))
        o_ref[...] = r.astype(o_ref.dtype)


class Rider(NamedTuple):
    tables: tuple
    args: tuple
    in_specs: tuple
    out_specs: tuple
    out_shape: tuple
    scratch_shapes: tuple
    n_fill: int
    body: Callable


def matmul_ws(x, w, l, out_dtype, tm, tn, n_out, w_row=None, sq_relu=False, side=None, rider=None,
              name="matmul_ws"):
    m, kd = x.shape
    nj, ni = n_out // tn, m // tm
    assert m % tm == 0 and n_out % tn == 0 and ni >= 2
    w_shape = (kd, tn) if w_row is None else (tn, kd)
    in_specs = [pl.BlockSpec((tm, kd), lambda j, i, *_: (i, 0)), pl.BlockSpec(memory_space=pl.ANY)]
    out_specs = [pl.BlockSpec((tm, tn), lambda j, i, *_: (i, j))]
    out_shape = [jax.ShapeDtypeStruct((m, n_out), out_dtype)]
    args = [x, w]
    scratch = [pltpu.VMEM(w_shape, F32), pltpu.VMEM(w_shape, BF16), pltpu.SemaphoreType.DMA(())]
    if side is not None:
        _, sr, sc = side.shape
        assert sr % (nj * ni * 16) == 0
        slab = sr // (nj * ni)
        in_specs.append(pl.BlockSpec((None, slab, sc), lambda j, i, *_: (l, j * ni + i, 0)))
        out_specs.append(pl.BlockSpec((slab, sc), lambda j, i, *_: (j * ni + i, 0)))
        out_shape.append(jax.ShapeDtypeStruct((sr, sc), BF16))
        args.append(side)
    tables = ()
    if rider is not None:
        tables = rider.tables
        in_specs += list(rider.in_specs)
        out_specs += list(rider.out_specs)
        out_shape += list(rider.out_shape)
        args += list(rider.args)
        scratch += list(rider.scratch_shapes)
    out = pl.pallas_call(
        functools.partial(_mm_ws_kernel, l=l, tn=tn, nj=nj, w_row=w_row, sq_relu=sq_relu,
                          has_side=side is not None, rider=rider),
        grid_spec=pltpu.PrefetchScalarGridSpec(
            num_scalar_prefetch=len(tables), grid=(nj, ni), in_specs=in_specs, out_specs=out_specs,
            scratch_shapes=scratch),
        out_shape=out_shape,
        compiler_params=_cparams(("arbitrary", "arbitrary")),
        name=name,
    )(*tables, *args)
    return out[0] if len(out) == 1 else tuple(out)


def _split_bf16(x):
    hi = x.astype(BF16)
    lo = (x - hi.astype(F32)).astype(BF16)
    return hi, lo


LOG2E = 1.4426950408889634


def _log2_sigmoid_pair(z2):
    lo, hi = jnp.minimum(z2, 0.0), jnp.maximum(z2, 0.0)
    neg_sp = jnp.log(1.0 + jnp.exp2(lo - hi)) * (-LOG2E)
    return lo + neg_sp, neg_sp - hi


def _dot_nt(a, b):
    return lax.dot_general(a, b, (((1,), (1,)), ((), ())), preferred_element_type=F32)


def _dot_tn(a, b):
    return lax.dot_general(a, b, (((0,), (0,)), ((), ())), preferred_element_type=F32)


def _sb_pair_body(step, tables, ins, outs, scratch, fill, *, tq):
    _, q_tab, k_tab, _ = tables
    q_ref, k_ref, v_ref = ins
    (o_ref,) = outs
    run_ref, acc_ref = scratch
    qi, kj = q_tab[step], k_tab[step]

    @pl.when(step == 0)
    def _():
        run_ref[...] = jnp.zeros_like(run_ref)
        acc_ref[...] = jnp.zeros_like(acc_ref)

    dh = SB_DH
    scale2 = dh ** -0.5 * LOG2E
    nsub = tq // LANE
    rr = lax.broadcasted_iota(jnp.int32, (LANE, 2 * LANE), 0)
    cc = lax.broadcasted_iota(jnp.int32, (LANE, 2 * LANE), 1)
    cum_rhs = ((cc >= LANE) | (rr > cc)).astype(BF16)
    cum_rhs = jnp.concatenate([cum_rhs, cum_rhs], axis=0)
    rows = lax.broadcasted_iota(jnp.int32, (tq, tq), 0)
    cols = lax.broadcasted_iota(jnp.int32, (tq, tq), 1)
    strict = (cols - rows) < (qi - kj) * tq
    first = kj == qi
    for h in range(SB_HEADS):
        hs = slice(h * dh, (h + 1) * dh)
        run = jnp.where(first, 0.0, run_ref[h])
        acc = jnp.where(first, 0.0, acc_ref[h])
        z2 = _dot_nt(q_ref[:, hs], k_ref[:, hs]) * scale2
        fill()
        log_beta, log_1m = _log2_sigmoid_pair(z2)
        log_1m = jnp.where(strict, log_1m, 0.0)
        hi, lo = _split_bf16(log_1m)
        after = [None] * nsub
        for sb in reversed(range(nsub)):
            cs = slice(sb * LANE, (sb + 1) * LANE)
            r2 = jnp.dot(jnp.concatenate([hi[:, cs], lo[:, cs]], axis=1), cum_rhs, preferred_element_type=F32)
            after[sb] = r2[:, :LANE] + run
            run = run + r2[:, LANE:]
        fill()
        w = jnp.where(strict, jnp.exp2(log_beta + jnp.concatenate(after, axis=1)), 0.0)
        acc = acc + jnp.dot(w.astype(BF16), v_ref[:, hs], preferred_element_type=F32)
        run_ref[h] = run
        acc_ref[h] = acc
        o_ref[:, hs] = acc.astype(o_ref.dtype)


def stickbreaking_rider(p_sb, batch, seq, nj, ni, tq=SB_TQ):
    assert SB_DH == LANE
    nq = seq // tq
    items = [(b, qi, kj, 1) for b in range(batch) for qi in range(nq) for kj in range(qi, -1, -1)]
    assert len(items) <= nj * ni
    items += [items[-1][:3] + (0,)] * (nj * ni - len(items))
    tables = tuple(jnp.asarray([it[c] for it in items], jnp.int32) for c in range(4))
    wd = SB_HEADS * SB_DH

    def spec(tab, col):
        return pl.BlockSpec((tq, wd), lambda j, i, bt, qt, kt, work: (
            bt[j * ni + i] * nq + (qt, kt)[tab][j * ni + i], col))

    return Rider(
        tables=tables, args=(p_sb, p_sb, p_sb),
        in_specs=(spec(0, 0), spec(1, 1), spec(1, 2)),
        out_specs=(spec(0, 0),),
        out_shape=(jax.ShapeDtypeStruct((batch * seq, wd), BF16),),
        scratch_shapes=(pltpu.VMEM((SB_HEADS, tq, LANE), F32), pltpu.VMEM((SB_HEADS, tq, LANE), F32)),
        n_fill=2 * SB_HEADS, body=functools.partial(_sb_pair_body, tq=tq))


def _gla_kernel(q_ref, k_ref, v_ref, g_ref, small_ref, up_ref, bias_ref, gain_ref, o_ref, st_ref):
    c = GLA_CHUNK

    @pl.when(pl.program_id(1) == 0)
    def _():
        st_ref[...] = jnp.zeros_like(st_ref)

    a_hi, a_lo = _split_bf16(small_ref[:, GA_LO:GA_LO + GLA_RANK])
    u_hi, u_lo = _split_bf16(up_ref[...])
    x = (jnp.dot(a_hi, u_hi, preferred_element_type=F32) + jnp.dot(a_hi, u_lo, preferred_element_type=F32)
         + jnp.dot(a_lo, u_hi, preferred_element_type=F32)) + bias_ref[...]
    log_a = _log2_sigmoid_pair(x * LOG2E)[0] / GLA_TAU
    r_i = lax.broadcasted_iota(jnp.int32, (c, c), 0)
    c_i = lax.broadcasted_iota(jnp.int32, (c, c), 1)
    tril = (c_i <= r_i).astype(BF16)
    g_hi, g_lo = _split_bf16(log_a)
    b_all = jnp.dot(tril, g_hi, preferred_element_type=F32) + jnp.dot(tril, g_lo, preferred_element_type=F32)

    sub = GLA_SUB
    lane = lax.broadcasted_iota(jnp.int32, (sub, LANE), 1)
    jrow = lax.broadcasted_iota(jnp.int32, (sub, LANE), 0)
    gain = gain_ref[...]

    for h in range(GLA_HEADS):
        b = b_all[:, h * GLA_DK:(h + 1) * GLA_DK]
        q = q_ref[:, h * GLA_DK:(h + 1) * GLA_DK] * (GLA_DK ** -0.5)
        k = k_ref[:, h * GLA_DK:(h + 1) * GLA_DK]
        v = v_ref[:, h * GLA_DV:(h + 1) * GLA_DV].astype(BF16)
        st = st_ref[h]
        o_inter = _dot_nt((q * jnp.exp2(b)).astype(BF16), st.astype(BF16))

        o_rows = []
        for blk in range(c // sub):
            lo_r, hi_r = blk * sub, (blk + 1) * sub
            b_blk, q_blk, k_blk = b[lo_r:hi_r], q[lo_r:hi_r], k[lo_r:hi_r]
            o_blk = o_inter[lo_r:hi_r]
            if blk > 0:
                b_first = b[lo_r:lo_r + 1]
                qt = (q_blk * jnp.exp2(b_blk - b_first)).astype(BF16)
                kt = (k[:lo_r] * jnp.exp2(b_first - b[:lo_r])).astype(BF16)
                a_off = _dot_nt(qt, kt)
                o_blk = o_blk + jnp.dot(a_off.astype(BF16), v[:lo_r], preferred_element_type=F32)
            a_t = jnp.zeros((sub, LANE), F32)
            for i in range(sub):
                e = jnp.exp2(jnp.minimum(b_blk[i:i + 1] - b_blk, 0.0))
                col = jnp.sum(q_blk[i:i + 1] * k_blk * e, axis=-1, keepdims=True)
                a_t = jnp.where(lane == i, col, a_t)
            a_t = jnp.where(jrow <= lane, a_t, 0.0)
            o_diag = _dot_tn(a_t.astype(BF16), v[lo_r:hi_r])
            o_rows.append(o_blk + o_diag[:sub])
        o = jnp.concatenate(o_rows, axis=0)

        b_last = b[c - 1:c]
        kd = (k * jnp.exp2(b_last - b)).astype(BF16)
        st_ref[h] = st * jnp.exp2(b_last) + _dot_tn(v, kd)

        o = o * lax.rsqrt(jnp.mean(o * o, axis=-1, keepdims=True) + EPS) * gain
        g = g_ref[:, h * GLA_DV:(h + 1) * GLA_DV]
        o = o * (g / (1.0 + jnp.exp(-g)))
        o_ref[:, h * GLA_DV:(h + 1) * GLA_DV] = o.astype(o_ref.dtype)


def gla(p_gla, p_small, gate_up, gate_bias, head_gain, batch, seq):
    c = GLA_CHUNK
    n = seq // c
    wq, wv = GLA_HEADS * GLA_DK, GLA_HEADS * GLA_DV
    return pl.pallas_call(
        _gla_kernel,
        grid=(batch, n),
        in_specs=[pl.BlockSpec((c, wq), lambda b, i: (b * n + i, 0)),
                  pl.BlockSpec((c, wq), lambda b, i: (b * n + i, 1)),
                  pl.BlockSpec((c, wv), lambda b, i: (b * n + i, 1)),
                  pl.BlockSpec((c, wv), lambda b, i: (b * n + i, 2)),
                  pl.BlockSpec((c, LANE), lambda b, i: (b * n + i, 0)),
                  pl.BlockSpec((GLA_RANK, wq), lambda b, i: (0, 0)),
                  pl.BlockSpec((1, wq), lambda b, i: (0, 0)),
                  pl.BlockSpec((1, GLA_DV), lambda b, i: (0, 0))],
        out_specs=pl.BlockSpec((c, wv), lambda b, i: (b * n + i, 0)),
        out_shape=jax.ShapeDtypeStruct((batch * seq, wv), BF16),
        scratch_shapes=[pltpu.VMEM((GLA_HEADS, GLA_DV, GLA_DK), F32)],
        compiler_params=_cparams(("parallel", "arbitrary")),
        name="gla",
    )(p_gla, p_gla, p_gla, p_gla, p_small, gate_up, gate_bias.reshape(1, wq), head_gain.reshape(1, GLA_DV))


def _bias_kernel(bucket_ref, rel_ref, o_ref):
    for h in range(DSA_HEADS):
        for off in range(3):
            bk = bucket_ref[off]

            def body(n, acc, bk=bk, h=h):
                return jnp.where(bk == n, rel_ref[n, h], acc)

            o_ref[h, off] = lax.fori_loop(0, N_BUCKETS, body, jnp.zeros(bk.shape, F32)) * LOG2E


def _rel_bucket(dist):
    max_exact = N_BUCKETS // 2
    d = jnp.maximum(dist, 1).astype(F32)
    large = max_exact + (jnp.log(d / max_exact) / math.log(MAX_DISTANCE / max_exact)
                         * (N_BUCKETS - max_exact)).astype(jnp.int32)
    large = jnp.minimum(large, N_BUCKETS - 1)
    return jnp.where(dist < max_exact, dist, large)


def bias_tiles(rel_bias):
    s = jnp.arange(Q_BLOCK, dtype=jnp.int32)[:, None]
    t = jnp.arange(Q_BLOCK, dtype=jnp.int32)[None, :]
    dist = jnp.stack([off * Q_BLOCK + t - s for off in range(3)])
    bucket = _rel_bucket(jnp.maximum(dist, 0))
    return pl.pallas_call(
        _bias_kernel,
        in_specs=[pl.BlockSpec(memory_space=pltpu.VMEM), pl.BlockSpec(memory_space=pltpu.SMEM)],
        out_specs=pl.BlockSpec(memory_space=pltpu.VMEM),
        out_shape=jax.ShapeDtypeStruct((DSA_HEADS, 3, Q_BLOCK, Q_BLOCK), F32),
        name="bias_tiles",
    )(bucket, rel_bias)


DSA_CHUNK = 4 * Q_BLOCK
NEG_BIG = -1e30


def _dsa_kernel(row_ref, smq_ref, sma_ref, dk_ref, dv_ref, bias_ref, o_ref,
                key_ref, neg_ref, q2_ref, acc_ref, *, topk):
    qb, ck = Q_BLOCK, DSA_CHUNK
    i = pl.program_id(1)
    bpc = ck // qb
    nck = (i + bpc) // bpc
    s_loc = lax.broadcasted_iota(jnp.int32, (ck, qb), 0)
    t_loc = lax.broadcasted_iota(jnp.int32, (ck, qb), 1)
    lane_k = lax.broadcasted_iota(jnp.int32, (1, LANE), 1)

    def visible(r0):
        return (r0 + s_loc) <= (i * qb + t_loc)

    def select_by_index():
        w_t = smq_ref[...].T * ((IDX_HEADS ** -0.5) * (IDX_DIM ** -0.5))
        for pr in range(IDX_HEADS // 2):
            qpair = row_ref[:, DSA_IQ + pr * LANE:DSA_IQ + (pr + 1) * LANE]
            zero = jnp.zeros_like(qpair)
            q2_ref[pr, :qb, :] = jnp.where(lane_k < IDX_DIM, qpair, zero)
            q2_ref[pr, qb:, :] = jnp.where(lane_k >= IDX_DIM, qpair, zero)

        def score_chunk(c, carry):
            r0 = pl.multiple_of(c * ck, ck)
            sm = sma_ref[pl.ds(r0, ck), :]
            k_dup = jnp.where(lane_k < IDX_DIM, pltpu.roll(sm, LANE - IK_LO, 1),
                              pltpu.roll(sm, IDX_DIM - IK_LO, 1)).astype(BF16)
            acc = jnp.zeros((ck, qb), F32)
            for pr in range(IDX_HEADS // 2):
                s2 = jnp.maximum(_dot_nt(k_dup, q2_ref[pr]), 0.0)
                he, ho = IW_LO + 2 * pr, IW_LO + 2 * pr + 1
                acc = acc + w_t[he:he + 1] * s2[:, :qb] + w_t[ho:ho + 1] * s2[:, qb:]
            val = jnp.where(visible(r0), acc + 0.0, -jnp.inf)
            bits = pltpu.bitcast(val, jnp.int32)
            key_ref[pl.ds(r0, ck), :] = jnp.where(bits < 0, bits ^ jnp.int32(0x7FFFFFFF), bits)
            return carry

        lax.fori_loop(0, nck, score_chunk, 0)

        kf = float(topk)

        def count(pred):
            def body(c, acc):
                r0 = pl.multiple_of(c * ck, ck)
                hit = jnp.where(pred(key_ref[pl.ds(r0, ck), :]), 1.0, 0.0)
                return acc + jnp.sum(hit.reshape(ck // 64, 64, qb), axis=0)
            acc = lax.fori_loop(0, nck, body, jnp.zeros((64, qb), F32))
            return jnp.sum(acc, axis=0, keepdims=True)

        int_min = jnp.int32(-2 ** 31)
        m0 = jnp.where(count(lambda kk: kk >= 0) >= kf, jnp.int32(0), int_min)

        def search(n, m):
            cand = m | lax.shift_left(jnp.int32(1), jnp.int32(30) - n)
            return jnp.where(count(lambda kk: kk >= cand) >= kf, cand, m)

        kth = lax.fori_loop(0, 31, search, m0)
        need = kf - count(lambda kk: kk > kth)

        tri_r = lax.broadcasted_iota(jnp.int32, (ck, ck), 0)
        tri_c = lax.broadcasted_iota(jnp.int32, (ck, ck), 1)
        tri = (tri_c <= tri_r).astype(BF16)

        def select_chunk(c, seen):
            r0 = pl.multiple_of(c * ck, ck)
            kblk = key_ref[pl.ds(r0, ck), :]
            eq = jnp.where(kblk == kth, 1.0, 0.0)
            rank = jnp.dot(tri, eq.astype(BF16), preferred_element_type=F32) + seen
            keep = jnp.where(kblk > kth, 1.0, jnp.where(rank <= need, eq, 0.0))
            keep = jnp.where(visible(r0), keep, 0.0)
            neg_ref[pl.ds(r0, ck), :] = jnp.where(keep > 0.0, 0.0, -jnp.inf)
            return seen + jnp.sum(eq, axis=0, keepdims=True)

        lax.fori_loop(0, nck, select_chunk, jnp.zeros((1, qb), F32))

    all_kept = (i + 1) * qb <= topk
    pl.when(jnp.logical_not(all_kept))(select_by_index)

    @pl.when(all_kept)
    def _():
        def body(c, carry):
            r0 = pl.multiple_of(c * ck, ck)
            neg_ref[pl.ds(r0, ck), :] = jnp.where(visible(r0), 0.0, -jnp.inf)
            return carry
        lax.fori_loop(0, nck, body, 0)

    acc_ref[...] = jnp.zeros_like(acc_ref)
    scale2 = DSA_DH ** -0.5 * LOG2E

    def attend_chunk(c, ml):
        r0 = pl.multiple_of(c * ck, ck)
        kc = dk_ref[pl.ds(r0, ck), :]
        vc = dv_ref[pl.ds(r0, ck), :]
        neg = neg_ref[pl.ds(r0, ck), :]
        offs = [jnp.clip(i - bpc * c - b, 0, 2) for b in range(bpc)]
        new_ml = []
        for h in range(DSA_HEADS):
            m_old, l_old = ml[h]
            bias = jnp.concatenate([bias_ref[h, off] for off in offs], axis=0)
            qh = row_ref[:, DSA_Q + h * DSA_DH:DSA_Q + (h + 1) * DSA_DH]
            lg = _dot_nt(kc, qh) * scale2 + bias + neg
            m_new = jnp.maximum(m_old, jnp.max(lg, axis=0, keepdims=True))
            alpha = jnp.exp2(m_old - m_new)
            pexp = jnp.exp2(lg - m_new)
            acc_ref[h] = alpha * acc_ref[h] + _dot_tn(vc, pexp.astype(BF16))
            new_ml.append((m_new, alpha * l_old + jnp.sum(pexp, axis=0, keepdims=True)))
        return tuple(new_ml)

    ml0 = tuple((jnp.full((1, qb), NEG_BIG, F32), jnp.zeros((1, qb), F32)) for _ in range(DSA_HEADS))
    ml = lax.fori_loop(0, nck, attend_chunk, ml0)
    for h in range(DSA_HEADS):
        o_t = acc_ref[h] / ml[h][1]
        o_ref[:, h * DSA_DH:(h + 1) * DSA_DH] = o_t.T.astype(o_ref.dtype)


def dsa(p_dsa, p_small, btiles, batch, seq):
    qb = Q_BLOCK
    nq = seq // qb
    assert seq % DSA_CHUNK == 0 and DSA_DH == LANE
    topk = min(TOPK_MAX, seq // 4)
    wq = DSA_HEADS * DSA_DH
    return pl.pallas_call(
        functools.partial(_dsa_kernel, topk=topk),
        grid=(batch, nq),
        in_specs=[pl.BlockSpec((qb, DSA_W), lambda b, i: (b * nq + i, 0)),
                  pl.BlockSpec((qb, LANE), lambda b, i: (b * nq + i, 1)),
                  pl.BlockSpec((seq, LANE), lambda b, i: (b, 1)),
                  pl.BlockSpec((seq, DSA_DH), lambda b, i: (b, DSA_K // DSA_DH)),
                  pl.BlockSpec((seq, DSA_DH), lambda b, i: (b, DSA_V // DSA_DH)),
                  pl.BlockSpec((DSA_HEADS, 3, qb, qb), lambda b, i: (0, 0, 0, 0))],
        out_specs=pl.BlockSpec((qb, wq), lambda b, i: (b * nq + i, 0)),
        out_shape=jax.ShapeDtypeStruct((batch * seq, wq), BF16),
        scratch_shapes=[pltpu.VMEM((seq, qb), jnp.int32), pltpu.VMEM((seq, qb), F32),
                        pltpu.VMEM((IDX_HEADS // 2, 2 * qb, LANE), BF16),
                        pltpu.VMEM((DSA_HEADS, DSA_DH, qb), F32)],
        compiler_params=_cparams(("parallel", "arbitrary")),
        name="dsa",
    )(p_dsa, p_small, p_small, p_dsa, p_dsa, btiles)


def _merge_kernel(oa_ref, ob_ref, oc_ref, ga_ref, gb_ref, gc_ref, w_ref, o_ref, *, tn):
    cols = pl.ds(pl.multiple_of(pl.program_id(1) * tn, LANE), tn)
    acc = None
    for br, (o_r, g_r) in enumerate(((oa_ref, ga_ref), (ob_ref, gb_ref), (oc_ref, gc_ref))):
        gate = 1.0 / (1.0 + jnp.exp(-g_r[...]))
        w = w_ref[br * BRANCH_WIDTH:(br + 1) * BRANCH_WIDTH, cols]
        term = gate * jnp.dot(o_r[...], w, preferred_element_type=F32)
        acc = term if acc is None else acc + term
    o_ref[...] = acc.astype(o_ref.dtype)


def gated_merge(o_a, o_b, o_c, p_gates, w_branch, tm=MERGE_TM, tn=MERGE_TN):
    m = o_a.shape[0]
    nj = D_MODEL // tn
    o_spec = pl.BlockSpec((tm, BRANCH_WIDTH), lambda i, j: (i, 0))

    def g_spec(br):
        return pl.BlockSpec((tm, tn), lambda i, j: (i, br * nj + j))

    w_spec = pl.BlockSpec(w_branch.shape, lambda i, j: (0, 0), pipeline_mode=pl.Buffered(1))
    return pl.pallas_call(
        functools.partial(_merge_kernel, tn=tn),
        grid=(m // tm, nj),
        in_specs=[o_spec, o_spec, o_spec, g_spec(0), g_spec(1), g_spec(2), w_spec],
        out_specs=pl.BlockSpec((tm, tn), lambda i, j: (i, j)),
        out_shape=jax.ShapeDtypeStruct((m, D_MODEL), BF16),
        compiler_params=_cparams(("parallel", "parallel")),
        name="gated_merge",
    )(o_a, o_b, o_c, p_gates, p_gates, p_gates, w_branch)


def mixer(h, x, l, btiles, norm_mix_post, norm_mlp_pre, w_in_t, gla_gate_up, gla_gate_bias, gla_head_gain,
          w_branch, w_out, batch, seq):
    def proj(src, out_dtype, name, tn=WS_TN, side=None, rider=None):
        off, width = src
        return matmul_ws(h, w_in_t, l, out_dtype, WS_TM, tn, width, w_row=lambda j: off + j * tn,
                         side=side, rider=rider, name=name)

    p_gla = proj(SRC_GLA, F32, "proj_gla")
    p_dsa = proj(SRC_DSA, BF16, "proj_dsa", tn=WS_TN // 2)
    p_sb = proj(SRC_SB, BF16, "proj_sb")
    sb = stickbreaking_rider(p_sb, batch, seq, SRC_GATES[1] // GATES_TN, h.shape[0] // WS_TM)
    p_gates, wb, o_c = proj(SRC_GATES, F32, "proj_gates_sb", tn=GATES_TN, rider=sb,
                            side=w_branch.reshape(DEPTH, N_BRANCHES * BRANCH_WIDTH, D_MODEL))
    p_small = matmul_ws(h, w_in_t, l, F32, WS_TM, LANE, 2 * LANE,
                        w_row=lambda j: SMALL_ROWS[0] + j * (SMALL_ROWS[1] - SMALL_ROWS[0]), name="proj_small")
    o_a = gla(p_gla, p_small, gla_gate_up[l], gla_gate_bias[l], gla_head_gain[l], batch, seq)
    o_b = dsa(p_dsa, p_small, btiles, batch, seq)
    merged = gated_merge(o_a, o_b, o_c, p_gates, wb)
    y = matmul_ws(merged, w_out, l, F32, WS_TM, WS_TN, D_MODEL, name="out_proj")
    return post_norm_residual(y, x, norm_mix_post[l], norm_mlp_pre[l])


def kernel(x, rel_bias, norm_mix_pre, norm_mix_post, norm_mlp_pre, norm_mlp_post, w_in, gla_gate_up,
           gla_gate_bias, gla_head_gain, w_branch, w_out, w_mlp_up, w_mlp_down):
    batch, seq, d = x.shape
    xf = x.reshape(batch * seq, d)
    btiles = bias_tiles(rel_bias)
    h = rmsnorm_cast(xf, norm_mix_pre[0])
    w_in_t = jnp.swapaxes(w_in, 1, 2)
    for l in range(DEPTH):
        xf, h = mixer(h, xf, l, btiles, norm_mix_post, norm_mlp_pre, w_in_t, gla_gate_up, gla_gate_bias,
                      gla_head_gain, w_branch, w_out, batch, seq)
        u, w_down = matmul_ws(h, w_mlp_up, l, BF16, WS_TM, WS_TN, D_FF, sq_relu=True, side=w_mlp_down,
                              name="mlp_up")
        y = matmul_ktiled(u, w_down, F32, DOWN_TM, DOWN_TN, DOWN_TK, name="mlp_down")
        g_next = norm_mix_pre[l + 1] if l + 1 < DEPTH else None
        xf, h = post_norm_residual(y, xf, norm_mlp_post[l], g_next)
    return xf.reshape(batch, seq, d)
```

```python
import functools
import math
from typing import Callable, NamedTuple

import jax
import jax.numpy as jnp
from jax import lax
from jax.experimental import pallas as pl
from jax.experimental.pallas import tpu as pltpu

F32 = jnp.float32
BF16 = jnp.bfloat16

D_MODEL = 4096
DEPTH = 2
N_BRANCHES = 3
BRANCH_WIDTH = 1024
GLA_HEADS, GLA_DK, GLA_DV, GLA_RANK, GLA_TAU, GLA_CHUNK = 4, 128, 256, 16, 16.0, 64
GLA_SUB = 16
DSA_HEADS, DSA_DH, IDX_HEADS, IDX_DIM, TOPK_MAX = 8, 128, 32, 64, 256
SB_HEADS, SB_DH = 8, 128
Q_BLOCK = 128
N_BUCKETS, MAX_DISTANCE = 32, 128
D_FF = 4 * D_MODEL
EPS = 1e-6
LANE = 128

WS_TM, WS_TN = 1024, 1024
GATES_TN = 512
DOWN_TM, DOWN_TN, DOWN_TK = 1024, 1024, 4096
MERGE_TM, MERGE_TN = 1024, 512
NORM_TM = 256
SB_TQ = 256
GLA_CHUNKS_PER_STEP = 4
VMEM_LIMIT = 60 * 1024 * 1024

SRC_GLA = (0, 3072)
SRC_GA = 3072
SRC_DSA = (3088, 3584)
SRC_IK = 6416
SRC_SB = (6512, 3072)
SRC_GATES = (9584, 3 * D_MODEL)
IN_COLS = SRC_GATES[0] + SRC_GATES[1]
DSA_Q, DSA_K, DSA_V, DSA_IQ, DSA_W = 0, 1024, 1152, 1280, 3328
SMALL_ROWS = (SRC_GA, SRC_IK - SRC_IK % LANE)
GA_LO, IK_LO, IW_LO = 0, SRC_IK % LANE, (SRC_IK + IDX_DIM) % LANE


def _cparams(sem):
    return pltpu.CompilerParams(dimension_semantics=sem, vmem_limit_bytes=VMEM_LIMIT)


def _rms_kernel(x_ref, g_ref, o_ref):
    x = x_ref[...]
    y = x * lax.rsqrt(jnp.mean(x * x, axis=-1, keepdims=True) + EPS)
    o_ref[...] = (y * g_ref[...]).astype(o_ref.dtype)


def rmsnorm_cast(x, g, tm=NORM_TM):
    m, d = x.shape
    return pl.pallas_call(
        _rms_kernel,
        grid=(m // tm,),
        in_specs=[pl.BlockSpec((tm, d), lambda i: (i, 0)), pl.BlockSpec((1, d), lambda i: (0, 0))],
        out_specs=pl.BlockSpec((tm, d), lambda i: (i, 0)),
        out_shape=jax.ShapeDtypeStruct((m, d), BF16),
        compiler_params=_cparams(("parallel",)),
        name="rmsnorm_cast",
    )(x, g.reshape(1, d))


def _post_kernel(y_ref, x_ref, gp_ref, gn_ref, xo_ref, ho_ref):
    y = y_ref[...]
    yn = y * lax.rsqrt(jnp.mean(y * y, axis=-1, keepdims=True) + EPS) * gp_ref[...]
    xn = x_ref[...] + yn
    xo_ref[...] = xn
    hn = xn * lax.rsqrt(jnp.mean(xn * xn, axis=-1, keepdims=True) + EPS) * gn_ref[...]
    ho_ref[...] = hn.astype(ho_ref.dtype)


def _post_last_kernel(y_ref, x_ref, gp_ref, xo_ref):
    y = y_ref[...]
    yn = y * lax.rsqrt(jnp.mean(y * y, axis=-1, keepdims=True) + EPS) * gp_ref[...]
    xo_ref[...] = x_ref[...] + yn


def post_norm_residual(y, x, g_post, g_next, tm=NORM_TM):
    m, d = x.shape
    row = pl.BlockSpec((tm, d), lambda i: (i, 0))
    vec = pl.BlockSpec((1, d), lambda i: (0, 0))
    if g_next is None:
        return pl.pallas_call(
            _post_last_kernel, grid=(m // tm,), in_specs=[row, row, vec], out_specs=row,
            out_shape=jax.ShapeDtypeStruct((m, d), F32),
            compiler_params=_cparams(("parallel",)), name="post_last",
        )(y, x, g_post.reshape(1, d)), None
    return pl.pallas_call(
        _post_kernel, grid=(m // tm,), in_specs=[row, row, vec, vec], out_specs=(row, row),
        out_shape=(jax.ShapeDtypeStruct((m, d), F32), jax.ShapeDtypeStruct((m, d), BF16)),
        compiler_params=_cparams(("parallel",)), name="post_norm",
    )(y, x, g_post.reshape(1, d), g_next.reshape(1, d))


def _mm_acc_kernel(x_ref, w_ref, o_ref, acc_ref, *, nk):
    k = pl.program_id(2)

    @pl.when(k == 0)
    def _():
        acc_ref[...] = jnp.zeros_like(acc_ref)

    acc_ref[...] += jnp.dot(x_ref[...], w_ref[...], preferred_element_type=F32)

    @pl.when(k == nk - 1)
    def _():
        o_ref[...] = acc_ref[...].astype(o_ref.dtype)


def matmul_ktiled(x, w, out_dtype, tm, tn, tk, name="matmul"):
    m, kd = x.shape
    _, n = w.shape
    assert m % tm == 0 and n % tn == 0 and kd % tk == 0
    nk = kd // tk
    return pl.pallas_call(
        functools.partial(_mm_acc_kernel, nk=nk),
        grid=(m // tm, n // tn, nk),
        in_specs=[pl.BlockSpec((tm, tk), lambda i, j, k: (i, k)),
                  pl.BlockSpec((tk, tn), lambda i, j, k: (k, j))],
        out_specs=pl.BlockSpec((tm, tn), lambda i, j, k: (i, j)),
        out_shape=jax.ShapeDtypeStruct((m, n), out_dtype),
        scratch_shapes=[pltpu.VMEM((tm, tn), F32)],
        compiler_params=_cparams(("parallel", "parallel", "arbitrary")),
        name=name,
    )(x, w)


def _mm_ws_kernel(*refs, l, tn, nj, w_row, sq_relu, has_side, rider):
    refs = list(refs)
    n_tab, n_rin, n_rout = (len(rider.tables), len(rider.in_specs), len(rider.out_specs)) if rider else (0, 0, 0)
    tables = [refs.pop(0) for _ in range(n_tab)]
    x_ref, w_hbm = refs.pop(0), refs.pop(0)
    side_ref = refs.pop(0) if has_side else None
    rider_ins = [refs.pop(0) for _ in range(n_rin)]
    o_ref = refs.pop(0)
    side_o_ref = refs.pop(0) if has_side else None
    rider_outs = [refs.pop(0) for _ in range(n_rout)]
    stage_ref, wb_ref, sem = refs.pop(0), refs.pop(0), refs.pop(0)
    rider_scratch = refs
    j, i = pl.program_id(0), pl.program_id(1)
    w_rows_are_outputs = w_row is not None
    if has_side:
        side_o_ref[...] = side_ref[...].astype(side_o_ref.dtype)

    def tile_copy(jt):
        if w_rows_are_outputs:
            src = w_hbm.at[l, pl.ds(pl.multiple_of(w_row(jt), 8), tn), :]
        else:
            src = w_hbm.at[l, :, pl.ds(pl.multiple_of(jt * tn, LANE), tn)]
        return pltpu.make_async_copy(src, stage_ref, sem)

    @pl.when((j == 0) & (i == 0))
    def _():
        tile_copy(0).start()

    @pl.when(i == 0)
    def _():
        tile_copy(j).wait()
        wb_ref[...] = stage_ref[...].astype(wb_ref.dtype)

    @pl.when((i == 1) & (j + 1 < nj))
    def _():
        tile_copy(j + 1).start()

    def product(ks):
        if w_rows_are_outputs:
            return lax.dot_general(x_ref[:, ks], wb_ref[:, ks], (((1,), (1,)), ((), ())),
                                   preferred_element_type=F32)
        return jnp.dot(x_ref[:, ks], wb_ref[ks, :], preferred_element_type=F32)

    if rider:
        assert o_ref.dtype == F32 and not sq_relu
        kd = x_ref.shape[1]
        slab = kd // rider.n_fill
        done = [0]

        def fill():
            s = done[0]
            done[0] += 1
            part = product(slice(s * slab, (s + 1) * slab))
            if s == 0:
                o_ref[...] = part
            else:
                o_ref[...] += part

        step = j * pl.num_programs(1) + i
        has_work = tables[-1][step] != 0

        @pl.when(has_work)
        def _():
            rider.body(step, tables, rider_ins, rider_outs, rider_scratch, fill)
            assert done[0] == rider.n_fill

        @pl.when(jnp.logical_not(has_work))
        def _():
            o_ref[...] = product(slice(None))
    else:
        r = product(slice(None))
        if sq_relu:
            r = jnp.square(jnp.maximum(r, 0.0))
        o_ref[...] = r.astype(o_ref.dtype)


class Rider(NamedTuple):
    tables: tuple
    args: tuple
    in_specs: tuple
    out_specs: tuple
    out_shape: tuple
    scratch_shapes: tuple
    n_fill: int
    body: Callable


def matmul_ws(x, w, l, out_dtype, tm, tn, n_out, w_row=None, sq_relu=False, side=None, rider=None,
              name="matmul_ws"):
    m, kd = x.shape
    nj, ni = n_out // tn, m // tm
    assert m % tm == 0 and n_out % tn == 0 and ni >= 2
    w_shape = (kd, tn) if w_row is None else (tn, kd)
    in_specs = [pl.BlockSpec((tm, kd), lambda j, i, *_: (i, 0)), pl.BlockSpec(memory_space=pl.ANY)]
    out_specs = [pl.BlockSpec((tm, tn), lambda j, i, *_: (i, j))]
    out_shape = [jax.ShapeDtypeStruct((m, n_out), out_dtype)]
    args = [x, w]
    scratch = [pltpu.VMEM(w_shape, F32), pltpu.VMEM(w_shape, BF16), pltpu.SemaphoreType.DMA(())]
    if side is not None:
        _, sr, sc = side.shape
        assert sr % (nj * ni * 16) == 0
        slab = sr // (nj * ni)
        in_specs.append(pl.BlockSpec((None, slab, sc), lambda j, i, *_: (l, j * ni + i, 0)))
        out_specs.append(pl.BlockSpec((slab, sc), lambda j, i, *_: (j * ni + i, 0)))
        out_shape.append(jax.ShapeDtypeStruct((sr, sc), BF16))
        args.append(side)
    tables = ()
    if rider is not None:
        tables = rider.tables
        in_specs += list(rider.in_specs)
        out_specs += list(rider.out_specs)
        out_shape += list(rider.out_shape)
        args += list(rider.args)
        scratch += list(rider.scratch_shapes)
    out = pl.pallas_call(
        functools.partial(_mm_ws_kernel, l=l, tn=tn, nj=nj, w_row=w_row, sq_relu=sq_relu,
                          has_side=side is not None, rider=rider),
        grid_spec=pltpu.PrefetchScalarGridSpec(
            num_scalar_prefetch=len(tables), grid=(nj, ni), in_specs=in_specs, out_specs=out_specs,
            scratch_shapes=scratch),
        out_shape=out_shape,
        compiler_params=_cparams(("arbitrary", "arbitrary")),
        name=name,
    )(*tables, *args)
    return out[0] if len(out) == 1 else tuple(out)


def _split_bf16(x):
    hi = x.astype(BF16)
    lo = (x - hi.astype(F32)).astype(BF16)
    return hi, lo


LOG2E = 1.4426950408889634


def _log2_sigmoid_pair(z2):
    lo, hi = jnp.minimum(z2, 0.0), jnp.maximum(z2, 0.0)
    neg_sp = jnp.log(1.0 + jnp.exp2(lo - hi)) * (-LOG2E)
    return lo + neg_sp, neg_sp - hi


def _dot_nt(a, b):
    return lax.dot_general(a, b, (((1,), (1,)), ((), ())), preferred_element_type=F32)


def _dot_tn(a, b):
    return lax.dot_general(a, b, (((0,), (0,)), ((), ())), preferred_element_type=F32)


def _sb_pair_body(step, tables, ins, outs, scratch, fill, *, tq):
    _, q_tab, k_tab, _ = tables
    q_ref, k_ref, v_ref = ins
    (o_ref,) = outs
    run_ref, acc_ref = scratch
    qi, kj = q_tab[step], k_tab[step]

    @pl.when(step == 0)
    def _():
        run_ref[...] = jnp.zeros_like(run_ref)
        acc_ref[...] = jnp.zeros_like(acc_ref)

    dh = SB_DH
    scale2 = dh ** -0.5 * LOG2E
    nsub = tq // LANE
    rr = lax.broadcasted_iota(jnp.int32, (LANE, 2 * LANE), 0)
    cc = lax.broadcasted_iota(jnp.int32, (LANE, 2 * LANE), 1)
    cum_rhs = ((cc >= LANE) | (rr > cc)).astype(BF16)
    cum_rhs = jnp.concatenate([cum_rhs, cum_rhs], axis=0)
    rows = lax.broadcasted_iota(jnp.int32, (tq, tq), 0)
    cols = lax.broadcasted_iota(jnp.int32, (tq, tq), 1)
    strict = (cols - rows) < (qi - kj) * tq
    first = kj == qi
    for h in range(SB_HEADS):
        hs = slice(h * dh, (h + 1) * dh)
        run = jnp.where(first, 0.0, run_ref[h])
        acc = jnp.where(first, 0.0, acc_ref[h])
        z2 = _dot_nt(q_ref[:, hs], k_ref[:, hs]) * scale2
        fill()
        log_beta, log_1m = _log2_sigmoid_pair(z2)
        log_1m = jnp.where(strict, log_1m, 0.0)
        hi, lo = _split_bf16(log_1m)
        after = [None] * nsub
        for sb in reversed(range(nsub)):
            cs = slice(sb * LANE, (sb + 1) * LANE)
            r2 = jnp.dot(jnp.concatenate([hi[:, cs], lo[:, cs]], axis=1), cum_rhs, preferred_element_type=F32)
            after[sb] = r2[:, :LANE] + run
            run = run + r2[:, LANE:]
        fill()
        w = jnp.where(strict, jnp.exp2(log_beta + jnp.concatenate(after, axis=1)), 0.0)
        acc = acc + jnp.dot(w.astype(BF16), v_ref[:, hs], preferred_element_type=F32)
        run_ref[h] = run
        acc_ref[h] = acc
        o_ref[:, hs] = acc.astype(o_ref.dtype)


def stickbreaking_rider(p_sb, batch, seq, nj, ni, tq=SB_TQ):
    assert SB_DH == LANE
    nq = seq // tq
    items = [(b, qi, kj, 1) for b in range(batch) for qi in range(nq) for kj in range(qi, -1, -1)]
    assert len(items) <= nj * ni
    items += [items[-1][:3] + (0,)] * (nj * ni - len(items))
    tables = tuple(jnp.asarray([it[c] for it in items], jnp.int32) for c in range(4))
    wd = SB_HEADS * SB_DH

    def spec(tab, col):
        return pl.BlockSpec((tq, wd), lambda j, i, bt, qt, kt, work: (
            bt[j * ni + i] * nq + (qt, kt)[tab][j * ni + i], col))

    return Rider(
        tables=tables, args=(p_sb, p_sb, p_sb),
        in_specs=(spec(0, 0), spec(1, 1), spec(1, 2)),
        out_specs=(spec(0, 0),),
        out_shape=(jax.ShapeDtypeStruct((batch * seq, wd), BF16),),
        scratch_shapes=(pltpu.VMEM((SB_HEADS, tq, LANE), F32), pltpu.VMEM((SB_HEADS, tq, LANE), F32)),
        n_fill=2 * SB_HEADS, body=functools.partial(_sb_pair_body, tq=tq))


def _gla_kernel(q_ref, k_ref, v_ref, g_ref, small_ref, up_ref, bias_ref, gain_ref, o_ref, st_ref):
    c = GLA_CHUNK

    @pl.when(pl.program_id(1) == 0)
    def _():
        st_ref[...] = jnp.zeros_like(st_ref)

    r_i = lax.broadcasted_iota(jnp.int32, (c, c), 0)
    c_i = lax.broadcasted_iota(jnp.int32, (c, c), 1)
    tril = (c_i <= r_i).astype(BF16)
    sub = GLA_SUB
    lane = lax.broadcasted_iota(jnp.int32, (sub, LANE), 1)
    jrow = lax.broadcasted_iota(jnp.int32, (sub, LANE), 0)
    gain = gain_ref[...]
    u_hi, u_lo = _split_bf16(up_ref[...])
    states = [st_ref[h] for h in range(GLA_HEADS)]
    for cc in range(q_ref.shape[0] // c):
        rs = slice(cc * c, (cc + 1) * c)
        states = _gla_chunk(rs, states, q_ref, k_ref, v_ref, g_ref, small_ref, bias_ref, o_ref,
                            u_hi, u_lo, tril, lane, jrow, gain)
    for h in range(GLA_HEADS):
        st_ref[h] = states[h]


def _gla_chunk(rs, states, q_ref, k_ref, v_ref, g_ref, small_ref, bias_ref, o_ref, u_hi, u_lo, tril, lane, jrow,
               gain):
    c, sub = GLA_CHUNK, GLA_SUB
    a_hi, a_lo = _split_bf16(small_ref[rs, GA_LO:GA_LO + GLA_RANK])
    x = (jnp.dot(a_hi, u_hi, preferred_element_type=F32) + jnp.dot(a_hi, u_lo, preferred_element_type=F32)
         + jnp.dot(a_lo, u_hi, preferred_element_type=F32)) + bias_ref[...]
    log_a = _log2_sigmoid_pair(x * LOG2E)[0] / GLA_TAU
    g_hi, g_lo = _split_bf16(log_a)
    b_all = jnp.dot(tril, g_hi, preferred_element_type=F32) + jnp.dot(tril, g_lo, preferred_element_type=F32)

    new_states = []
    for h in range(GLA_HEADS):
        b = b_all[:, h * GLA_DK:(h + 1) * GLA_DK]
        q = q_ref[rs, h * GLA_DK:(h + 1) * GLA_DK] * (GLA_DK ** -0.5)
        k = k_ref[rs, h * GLA_DK:(h + 1) * GLA_DK]
        v = v_ref[rs, h * GLA_DV:(h + 1) * GLA_DV].astype(BF16)
        st = states[h]
        o_inter = _dot_nt((q * jnp.exp2(b)).astype(BF16), st.astype(BF16))

        o_rows = []
        for blk in range(c // sub):
            lo_r, hi_r = blk * sub, (blk + 1) * sub
            b_blk, q_blk, k_blk = b[lo_r:hi_r], q[lo_r:hi_r], k[lo_r:hi_r]
            o_blk = o_inter[lo_r:hi_r]
            if blk > 0:
                b_first = b[lo_r:lo_r + 1]
                qt = (q_blk * jnp.exp2(b_blk - b_first)).astype(BF16)
                kt = (k[:lo_r] * jnp.exp2(b_first - b[:lo_r])).astype(BF16)
                a_off = _dot_nt(qt, kt)
                o_blk = o_blk + jnp.dot(a_off.astype(BF16), v[:lo_r], preferred_element_type=F32)
            a_t = jnp.zeros((sub, LANE), F32)
            for i in range(sub):
                e = jnp.exp2(jnp.minimum(b_blk[i:i + 1] - b_blk, 0.0))
                col = jnp.sum(q_blk[i:i + 1] * k_blk * e, axis=-1, keepdims=True)
                a_t = jnp.where(lane == i, col, a_t)
            a_t = jnp.where(jrow <= lane, a_t, 0.0)
            o_diag = _dot_tn(a_t.astype(BF16), v[lo_r:hi_r])
            o_rows.append(o_blk + o_diag[:sub])
        o = jnp.concatenate(o_rows, axis=0)

        b_last = b[c - 1:c]
        kd = (k * jnp.exp2(b_last - b)).astype(BF16)
        new_states.append(st * jnp.exp2(b_last) + _dot_tn(v, kd))

        o = o * lax.rsqrt(jnp.mean(o * o, axis=-1, keepdims=True) + EPS) * gain
        g = g_ref[rs, h * GLA_DV:(h + 1) * GLA_DV]
        o = o * (g / (1.0 + jnp.exp(-g)))
        o_ref[rs, h * GLA_DV:(h + 1) * GLA_DV] = o.astype(o_ref.dtype)
    return new_states


def gla(p_gla, p_small, gate_up, gate_bias, head_gain, batch, seq):
    c = GLA_CHUNK * GLA_CHUNKS_PER_STEP
    n = seq // c
    wq, wv = GLA_HEADS * GLA_DK, GLA_HEADS * GLA_DV
    return pl.pallas_call(
        _gla_kernel,
        grid=(batch, n),
        in_specs=[pl.BlockSpec((c, wq), lambda b, i: (b * n + i, 0)),
                  pl.BlockSpec((c, wq), lambda b, i: (b * n + i, 1)),
                  pl.BlockSpec((c, wv), lambda b, i: (b * n + i, 1)),
                  pl.BlockSpec((c, wv), lambda b, i: (b * n + i, 2)),
                  pl.BlockSpec((c, LANE), lambda b, i: (b * n + i, 0)),
                  pl.BlockSpec((GLA_RANK, wq), lambda b, i: (0, 0)),
                  pl.BlockSpec((1, wq), lambda b, i: (0, 0)),
                  pl.BlockSpec((1, GLA_DV), lambda b, i: (0, 0))],
        out_specs=pl.BlockSpec((c, wv), lambda b, i: (b * n + i, 0)),
        out_shape=jax.ShapeDtypeStruct((batch * seq, wv), BF16),
        scratch_shapes=[pltpu.VMEM((GLA_HEADS, GLA_DV, GLA_DK), F32)],
        compiler_params=_cparams(("parallel", "arbitrary")),
        name="gla",
    )(p_gla, p_gla, p_gla, p_gla, p_small, gate_up, gate_bias.reshape(1, wq), head_gain.reshape(1, GLA_DV))


def _bias_kernel(bucket_ref, rel_ref, o_ref):
    for h in range(DSA_HEADS):
        for off in range(3):
            bk = bucket_ref[off]

            def body(n, acc, bk=bk, h=h):
                return jnp.where(bk == n, rel_ref[n, h], acc)

            o_ref[h, off] = lax.fori_loop(0, N_BUCKETS, body, jnp.zeros(bk.shape, F32)) * LOG2E


def _rel_bucket(dist):
    max_exact = N_BUCKETS // 2
    d = jnp.maximum(dist, 1).astype(F32)
    large = max_exact + (jnp.log(d / max_exact) / math.log(MAX_DISTANCE / max_exact)
                         * (N_BUCKETS - max_exact)).astype(jnp.int32)
    large = jnp.minimum(large, N_BUCKETS - 1)
    return jnp.where(dist < max_exact, dist, large)


def bias_tiles(rel_bias):
    s = jnp.arange(Q_BLOCK, dtype=jnp.int32)[:, None]
    t = jnp.arange(Q_BLOCK, dtype=jnp.int32)[None, :]
    dist = jnp.stack([off * Q_BLOCK + t - s for off in range(3)])
    bucket = _rel_bucket(jnp.maximum(dist, 0))
    return pl.pallas_call(
        _bias_kernel,
        in_specs=[pl.BlockSpec(memory_space=pltpu.VMEM), pl.BlockSpec(memory_space=pltpu.SMEM)],
        out_specs=pl.BlockSpec(memory_space=pltpu.VMEM),
        out_shape=jax.ShapeDtypeStruct((DSA_HEADS, 3, Q_BLOCK, Q_BLOCK), F32),
        name="bias_tiles",
    )(bucket, rel_bias)


DSA_CHUNK = 4 * Q_BLOCK
NEG_BIG = float(jnp.finfo(jnp.float32).min)


def _dsa_kernel(row_ref, smq_ref, sma_ref, dk_ref, dv_ref, bias_ref, o_ref,
                key_ref, neg_ref, q2_ref, acc_ref, *, topk):
    qb, ck = Q_BLOCK, DSA_CHUNK
    i = pl.program_id(1)
    bpc = ck // qb
    nck = (i + bpc) // bpc
    s_loc = lax.broadcasted_iota(jnp.int32, (ck, qb), 0)
    t_loc = lax.broadcasted_iota(jnp.int32, (ck, qb), 1)
    lane_k = lax.broadcasted_iota(jnp.int32, (1, LANE), 1)

    def visible(r0):
        return (r0 + s_loc) <= (i * qb + t_loc)

    def select_by_index():
        w_t = smq_ref[...].T * ((IDX_HEADS ** -0.5) * (IDX_DIM ** -0.5))
        for pr in range(IDX_HEADS // 2):
            qpair = row_ref[:, DSA_IQ + pr * LANE:DSA_IQ + (pr + 1) * LANE]
            zero = jnp.zeros_like(qpair)
            q2_ref[pr, :qb, :] = jnp.where(lane_k < IDX_DIM, qpair, zero)
            q2_ref[pr, qb:, :] = jnp.where(lane_k >= IDX_DIM, qpair, zero)

        def score_chunk(c, carry):
            r0 = pl.multiple_of(c * ck, ck)
            sm = sma_ref[pl.ds(r0, ck), :]
            k_dup = jnp.where(lane_k < IDX_DIM, pltpu.roll(sm, LANE - IK_LO, 1),
                              pltpu.roll(sm, IDX_DIM - IK_LO, 1)).astype(BF16)
            acc = jnp.zeros((ck, qb), F32)
            for pr in range(IDX_HEADS // 2):
                s2 = jnp.maximum(_dot_nt(k_dup, q2_ref[pr]), 0.0)
                he, ho = IW_LO + 2 * pr, IW_LO + 2 * pr + 1
                acc = acc + w_t[he:he + 1] * s2[:, :qb] + w_t[ho:ho + 1] * s2[:, qb:]
            val = jnp.where(visible(r0), acc + 0.0, -jnp.inf)
            bits = pltpu.bitcast(val, jnp.int32)
            key_ref[pl.ds(r0, ck), :] = jnp.where(bits < 0, bits ^ jnp.int32(0x7FFFFFFF), bits)
            return carry

        lax.fori_loop(0, nck, score_chunk, 0)

        kf = float(topk)

        def count(pred):
            def body(c, acc):
                r0 = pl.multiple_of(c * ck, ck)
                hit = jnp.where(pred(key_ref[pl.ds(r0, ck), :]), 1.0, 0.0)
                return acc + jnp.sum(hit.reshape(ck // 64, 64, qb), axis=0)
            acc = lax.fori_loop(0, nck, body, jnp.zeros((64, qb), F32))
            return jnp.sum(acc, axis=0, keepdims=True)

        int_min = jnp.int32(-2 ** 31)
        m0 = jnp.where(count(lambda kk: kk >= 0) >= kf, jnp.int32(0), int_min)

        def search(n, m):
            cand = m | lax.shift_left(jnp.int32(1), jnp.int32(30) - n)
            return jnp.where(count(lambda kk: kk >= cand) >= kf, cand, m)

        kth = lax.fori_loop(0, 31, search, m0)
        need = kf - count(lambda kk: kk > kth)

        tri_r = lax.broadcasted_iota(jnp.int32, (ck, ck), 0)
        tri_c = lax.broadcasted_iota(jnp.int32, (ck, ck), 1)
        tri = (tri_c <= tri_r).astype(BF16)

        def select_chunk(c, seen):
            r0 = pl.multiple_of(c * ck, ck)
            kblk = key_ref[pl.ds(r0, ck), :]
            eq = jnp.where(kblk == kth, 1.0, 0.0)
            rank = jnp.dot(tri, eq.astype(BF16), preferred_element_type=F32) + seen
            keep = jnp.where(kblk > kth, 1.0, jnp.where(rank <= need, eq, 0.0))
            keep = jnp.where(visible(r0), keep, 0.0)
            neg_ref[pl.ds(r0, ck), :] = jnp.where(keep > 0.0, 0.0, -jnp.inf)
            return seen + jnp.sum(eq, axis=0, keepdims=True)

        lax.fori_loop(0, nck, select_chunk, jnp.zeros((1, qb), F32))

    all_kept = (i + 1) * qb <= topk
    pl.when(jnp.logical_not(all_kept))(select_by_index)

    @pl.when(all_kept)
    def _():
        def body(c, carry):
            r0 = pl.multiple_of(c * ck, ck)
            neg_ref[pl.ds(r0, ck), :] = jnp.where(visible(r0), 0.0, -jnp.inf)
            return carry
        lax.fori_loop(0, nck, body, 0)

    acc_ref[...] = jnp.zeros_like(acc_ref)
    scale2 = DSA_DH ** -0.5 * LOG2E

    def attend_chunk(c, ml):
        r0 = pl.multiple_of(c * ck, ck)
        kc = dk_ref[pl.ds(r0, ck), :]
        vc = dv_ref[pl.ds(r0, ck), :]
        neg = neg_ref[pl.ds(r0, ck), :]
        offs = [jnp.clip(i - bpc * c - b, 0, 2) for b in range(bpc)]
        new_ml = []
        for h in range(DSA_HEADS):
            m_old, l_old = ml[h]
            bias = jnp.concatenate([bias_ref[h, off] for off in offs], axis=0)
            qh = row_ref[:, DSA_Q + h * DSA_DH:DSA_Q + (h + 1) * DSA_DH]
            lg = _dot_nt(kc, qh) * scale2 + bias + neg
            m_new = jnp.maximum(m_old, jnp.max(lg, axis=0, keepdims=True))
            alpha = jnp.exp2(m_old - m_new)
            pexp = jnp.exp2(lg - m_new)
            acc_ref[h] = alpha * acc_ref[h] + _dot_tn(vc, pexp.astype(BF16))
            new_ml.append((m_new, alpha * l_old + jnp.sum(pexp, axis=0, keepdims=True)))
        return tuple(new_ml)

    ml0 = tuple((jnp.full((1, qb), NEG_BIG, F32), jnp.zeros((1, qb), F32)) for _ in range(DSA_HEADS))
    ml = lax.fori_loop(0, nck, attend_chunk, ml0)
    for h in range(DSA_HEADS):
        o_t = acc_ref[h] / ml[h][1]
        o_ref[:, h * DSA_DH:(h + 1) * DSA_DH] = o_t.T.astype(o_ref.dtype)


def dsa(p_dsa, p_small, btiles, batch, seq):
    qb = Q_BLOCK
    nq = seq // qb
    assert seq % DSA_CHUNK == 0 and DSA_DH == LANE
    topk = min(TOPK_MAX, seq // 4)
    wq = DSA_HEADS * DSA_DH
    return pl.pallas_call(
        functools.partial(_dsa_kernel, topk=topk),
        grid=(batch, nq),
        in_specs=[pl.BlockSpec((qb, DSA_W), lambda b, i: (b * nq + i, 0)),
                  pl.BlockSpec((qb, LANE), lambda b, i: (b * nq + i, 1)),
                  pl.BlockSpec((seq, LANE), lambda b, i: (b, 1)),
                  pl.BlockSpec((seq, DSA_DH), lambda b, i: (b, DSA_K // DSA_DH)),
                  pl.BlockSpec((seq, DSA_DH), lambda b, i: (b, DSA_V // DSA_DH)),
                  pl.BlockSpec((DSA_HEADS, 3, qb, qb), lambda b, i: (0, 0, 0, 0))],
        out_specs=pl.BlockSpec((qb, wq), lambda b, i: (b * nq + i, 0)),
        out_shape=jax.ShapeDtypeStruct((batch * seq, wq), BF16),
        scratch_shapes=[pltpu.VMEM((seq, qb), jnp.int32), pltpu.VMEM((seq, qb), F32),
                        pltpu.VMEM((IDX_HEADS // 2, 2 * qb, LANE), BF16),
                        pltpu.VMEM((DSA_HEADS, DSA_DH, qb), F32)],
        compiler_params=_cparams(("parallel", "arbitrary")),
        name="dsa",
    )(p_dsa, p_small, p_small, p_dsa, p_dsa, btiles)


def _merge_kernel(oa_ref, ob_ref, oc_ref, ga_ref, gb_ref, gc_ref, w_ref, o_ref, *, tn):
    cols = pl.ds(pl.multiple_of(pl.program_id(1) * tn, LANE), tn)
    acc = None
    for br, (o_r, g_r) in enumerate(((oa_ref, ga_ref), (ob_ref, gb_ref), (oc_ref, gc_ref))):
        gate = 1.0 / (1.0 + jnp.exp(-g_r[...]))
        w = w_ref[br * BRANCH_WIDTH:(br + 1) * BRANCH_WIDTH, cols]
        term = gate * jnp.dot(o_r[...], w, preferred_element_type=F32)
        acc = term if acc is None else acc + term
    o_ref[...] = acc.astype(o_ref.dtype)


def gated_merge(o_a, o_b, o_c, p_gates, w_branch, tm=MERGE_TM, tn=MERGE_TN):
    m = o_a.shape[0]
    nj = D_MODEL // tn
    o_spec = pl.BlockSpec((tm, BRANCH_WIDTH), lambda i, j: (i, 0))

    def g_spec(br):
        return pl.BlockSpec((tm, tn), lambda i, j: (i, br * nj + j))

    w_spec = pl.BlockSpec(w_branch.shape, lambda i, j: (0, 0), pipeline_mode=pl.Buffered(1))
    return pl.pallas_call(
        functools.partial(_merge_kernel, tn=tn),
        grid=(m // tm, nj),
        in_specs=[o_spec, o_spec, o_spec, g_spec(0), g_spec(1), g_spec(2), w_spec],
        out_specs=pl.BlockSpec((tm, tn), lambda i, j: (i, j)),
        out_shape=jax.ShapeDtypeStruct((m, D_MODEL), BF16),
        compiler_params=_cparams(("parallel", "parallel")),
        name="gated_merge",
    )(o_a, o_b, o_c, p_gates, p_gates, p_gates, w_branch)


def mixer(h, x, l, btiles, norm_mix_post, norm_mlp_pre, w_in_t, gla_gate_up, gla_gate_bias, gla_head_gain,
          w_branch, w_out, batch, seq):
    def proj(src, out_dtype, name, tn=WS_TN, side=None, rider=None):
        off, width = src
        return matmul_ws(h, w_in_t, l, out_dtype, WS_TM, tn, width, w_row=lambda j: off + j * tn,
                         side=side, rider=rider, name=name)

    p_gla = proj(SRC_GLA, F32, "proj_gla")
    p_dsa = proj(SRC_DSA, BF16, "proj_dsa", tn=WS_TN // 2)
    p_sb = proj(SRC_SB, BF16, "proj_sb")
    sb = stickbreaking_rider(p_sb, batch, seq, SRC_GATES[1] // GATES_TN, h.shape[0] // WS_TM)
    p_gates, wb, o_c = proj(SRC_GATES, F32, "proj_gates_sb", tn=GATES_TN, rider=sb,
                            side=w_branch.reshape(DEPTH, N_BRANCHES * BRANCH_WIDTH, D_MODEL))
    p_small = matmul_ws(h, w_in_t, l, F32, WS_TM, LANE, 2 * LANE,
                        w_row=lambda j: SMALL_ROWS[0] + j * (SMALL_ROWS[1] - SMALL_ROWS[0]), name="proj_small")
    o_a = gla(p_gla, p_small, gla_gate_up[l], gla_gate_bias[l], gla_head_gain[l], batch, seq)
    o_b = dsa(p_dsa, p_small, btiles, batch, seq)
    merged = gated_merge(o_a, o_b, o_c, p_gates, wb)
    y = matmul_ws(merged, w_out, l, F32, WS_TM, WS_TN, D_MODEL, name="out_proj")
    return post_norm_residual(y, x, norm_mix_post[l], norm_mlp_pre[l])


def kernel(x, rel_bias, norm_mix_pre, norm_mix_post, norm_mlp_pre, norm_mlp_post, w_in, gla_gate_up,
           gla_gate_bias, gla_head_gain, w_branch, w_out, w_mlp_up, w_mlp_down):
    batch, seq, d = x.shape
    xf = x.reshape(batch * seq, d)
    btiles = bias_tiles(rel_bias)
    h = rmsnorm_cast(xf, norm_mix_pre[0])
    w_in_t = jnp.swapaxes(w_in, 1, 2)
    for l in range(DEPTH):
        xf, h = mixer(h, xf, l, btiles, norm_mix_post, norm_mlp_pre, w_in_t, gla_gate_up, gla_gate_bias,
                      gla_head_gain, w_branch, w_out, batch, seq)
        u, w_down = matmul_ws(h, w_mlp_up, l, BF16, WS_TM, WS_TN, D_FF, sq_relu=True, side=w_mlp_down,
                              name="mlp_up")
        y = matmul_ktiled(u, w_down, F32, DOWN_TM, DOWN_TN, DOWN_TK, name="mlp_down")
        g_next = norm_mix_pre[l + 1] if l + 1 < DEPTH else None
        xf, h = post_norm_residual(y, xf, norm_mlp_post[l], g_next)
    return xf.reshape(batch, seq, d)
```

```python
import functools
import math
from typing import Callable, NamedTuple

import jax
import jax.numpy as jnp
from jax import lax
from jax.experimental import pallas as pl
from jax.experimental.pallas import tpu as pltpu

F32 = jnp.float32
BF16 = jnp.bfloat16

D_MODEL = 4096
DEPTH = 2
N_BRANCHES = 3
BRANCH_WIDTH = 1024
GLA_HEADS, GLA_DK, GLA_DV, GLA_RANK, GLA_TAU, GLA_CHUNK = 4, 128, 256, 16, 16.0, 64
GLA_SUB = 16
DSA_HEADS, DSA_DH, IDX_HEADS, IDX_DIM, TOPK_MAX = 8, 128, 32, 64, 256
SB_HEADS, SB_DH = 8, 128
Q_BLOCK = 128
N_BUCKETS, MAX_DISTANCE = 32, 128
D_FF = 4 * D_MODEL
EPS = 1e-6
LANE = 128

WS_TM, WS_TN = 1024, 1024
GATES_TN = 512
DOWN_TM, DOWN_TN, DOWN_TK = 1024, 1024, 4096
MERGE_TM, MERGE_TN = 1024, 512
NORM_TM = 256
SB_TQ = 256
GLA_CHUNKS_PER_STEP = 4
VMEM_LIMIT = 60 * 1024 * 1024

SRC_GLA = (0, 3072)
SRC_GA = 3072
SRC_DSA = (3088, 3584)
SRC_IK = 6416
SRC_SB = (6512, 3072)
SRC_GATES = (9584, 3 * D_MODEL)
IN_COLS = SRC_GATES[0] + SRC_GATES[1]
DSA_Q, DSA_K, DSA_V, DSA_IQ, DSA_W = 0, 1024, 1152, 1280, 3328
SMALL_ROWS = (SRC_GA, SRC_IK - SRC_IK % LANE)
GA_LO, IK_LO, IW_LO = 0, SRC_IK % LANE, (SRC_IK + IDX_DIM) % LANE


def _cparams(sem):
    return pltpu.CompilerParams(dimension_semantics=sem, vmem_limit_bytes=VMEM_LIMIT)


def _rms_kernel(x_ref, g_ref, o_ref):
    x = x_ref[...]
    y = x * lax.rsqrt(jnp.mean(x * x, axis=-1, keepdims=True) + EPS)
    o_ref[...] = (y * g_ref[...]).astype(o_ref.dtype)


def rmsnorm_cast(x, g, tm=NORM_TM):
    m, d = x.shape
    return pl.pallas_call(
        _rms_kernel,
        grid=(m // tm,),
        in_specs=[pl.BlockSpec((tm, d), lambda i: (i, 0)), pl.BlockSpec((1, d), lambda i: (0, 0))],
        out_specs=pl.BlockSpec((tm, d), lambda i: (i, 0)),
        out_shape=jax.ShapeDtypeStruct((m, d), BF16),
        compiler_params=_cparams(("parallel",)),
        name="rmsnorm_cast",
    )(x, g.reshape(1, d))


def _post_kernel(y_ref, x_ref, gp_ref, gn_ref, xo_ref, ho_ref):
    y = y_ref[...]
    yn = y * lax.rsqrt(jnp.mean(y * y, axis=-1, keepdims=True) + EPS) * gp_ref[...]
    xn = x_ref[...] + yn
    xo_ref[...] = xn
    hn = xn * lax.rsqrt(jnp.mean(xn * xn, axis=-1, keepdims=True) + EPS) * gn_ref[...]
    ho_ref[...] = hn.astype(ho_ref.dtype)


def _post_last_kernel(y_ref, x_ref, gp_ref, xo_ref):
    y = y_ref[...]
    yn = y * lax.rsqrt(jnp.mean(y * y, axis=-1, keepdims=True) + EPS) * gp_ref[...]
    xo_ref[...] = x_ref[...] + yn


def post_norm_residual(y, x, g_post, g_next, tm=NORM_TM):
    m, d = x.shape
    row = pl.BlockSpec((tm, d), lambda i: (i, 0))
    vec = pl.BlockSpec((1, d), lambda i: (0, 0))
    if g_next is None:
        return pl.pallas_call(
            _post_last_kernel, grid=(m // tm,), in_specs=[row, row, vec], out_specs=row,
            out_shape=jax.ShapeDtypeStruct((m, d), F32),
            compiler_params=_cparams(("parallel",)), name="post_last",
        )(y, x, g_post.reshape(1, d)), None
    return pl.pallas_call(
        _post_kernel, grid=(m // tm,), in_specs=[row, row, vec, vec], out_specs=(row, row),
        out_shape=(jax.ShapeDtypeStruct((m, d), F32), jax.ShapeDtypeStruct((m, d), BF16)),
        compiler_params=_cparams(("parallel",)), name="post_norm",
    )(y, x, g_post.reshape(1, d), g_next.reshape(1, d))


def _mm_acc_kernel(x_ref, w_ref, o_ref, acc_ref, *, nk):
    k = pl.program_id(2)

    @pl.when(k == 0)
    def _():
        acc_ref[...] = jnp.zeros_like(acc_ref)

    acc_ref[...] += jnp.dot(x_ref[...], w_ref[...], preferred_element_type=F32)

    @pl.when(k == nk - 1)
    def _():
        o_ref[...] = acc_ref[...].astype(o_ref.dtype)


def matmul_ktiled(x, w, out_dtype, tm, tn, tk, name="matmul"):
    m, kd = x.shape
    _, n = w.shape
    assert m % tm == 0 and n % tn == 0 and kd % tk == 0
    nk = kd // tk
    return pl.pallas_call(
        functools.partial(_mm_acc_kernel, nk=nk),
        grid=(m // tm, n // tn, nk),
        in_specs=[pl.BlockSpec((tm, tk), lambda i, j, k: (i, k)),
                  pl.BlockSpec((tk, tn), lambda i, j, k: (k, j))],
        out_specs=pl.BlockSpec((tm, tn), lambda i, j, k: (i, j)),
        out_shape=jax.ShapeDtypeStruct((m, n), out_dtype),
        scratch_shapes=[pltpu.VMEM((tm, tn), F32)],
        compiler_params=_cparams(("parallel", "parallel", "arbitrary")),
        name=name,
    )(x, w)


def _mm_ws_kernel(*refs, l, tn, nj, w_row, sq_relu, has_side, rider):
    refs = list(refs)
    n_tab, n_rin, n_rout = (len(rider.tables), len(rider.in_specs), len(rider.out_specs)) if rider else (0, 0, 0)
    tables = [refs.pop(0) for _ in range(n_tab)]
    x_ref, w_hbm = refs.pop(0), refs.pop(0)
    side_ref = refs.pop(0) if has_side else None
    rider_ins = [refs.pop(0) for _ in range(n_rin)]
    o_ref = refs.pop(0)
    side_o_ref = refs.pop(0) if has_side else None
    rider_outs = [refs.pop(0) for _ in range(n_rout)]
    stage_ref, wb_ref, sem = refs.pop(0), refs.pop(0), refs.pop(0)
    rider_scratch = refs
    j, i = pl.program_id(0), pl.program_id(1)
    w_rows_are_outputs = w_row is not None
    if has_side:
        side_o_ref[...] = side_ref[...].astype(side_o_ref.dtype)

    def tile_copy(jt):
        if w_rows_are_outputs:
            src = w_hbm.at[l, pl.ds(pl.multiple_of(w_row(jt), 8), tn), :]
        else:
            src = w_hbm.at[l, :, pl.ds(pl.multiple_of(jt * tn, LANE), tn)]
        return pltpu.make_async_copy(src, stage_ref, sem)

    @pl.when((j == 0) & (i == 0))
    def _():
        tile_copy(0).start(priority=1)

    @pl.when(i == 0)
    def _():
        tile_copy(j).wait()
        wb_ref[...] = stage_ref[...].astype(wb_ref.dtype)

    @pl.when((i == 1) & (j + 1 < nj))
    def _():
        tile_copy(j + 1).start(priority=1)

    def product(ks):
        if w_rows_are_outputs:
            return lax.dot_general(x_ref[:, ks], wb_ref[:, ks], (((1,), (1,)), ((), ())),
                                   preferred_element_type=F32)
        return jnp.dot(x_ref[:, ks], wb_ref[ks, :], preferred_element_type=F32)

    if rider:
        assert o_ref.dtype == F32 and not sq_relu
        kd = x_ref.shape[1]
        slab = kd // rider.n_fill
        done = [0]

        def fill():
            s = done[0]
            done[0] += 1
            part = product(slice(s * slab, (s + 1) * slab))
            if s == 0:
                o_ref[...] = part
            else:
                o_ref[...] += part

        step = j * pl.num_programs(1) + i
        has_work = tables[-1][step] != 0

        @pl.when(has_work)
        def _():
            rider.body(step, tables, rider_ins, rider_outs, rider_scratch, fill)
            assert done[0] == rider.n_fill

        @pl.when(jnp.logical_not(has_work))
        def _():
            o_ref[...] = product(slice(None))
    else:
        r = product(slice(None))
        if sq_relu:
            r = jnp.square(jnp.maximum(r, 0.0))
        o_ref[...] = r.astype(o_ref.dtype)


class Rider(NamedTuple):
    tables: tuple
    args: tuple
    in_specs: tuple
    out_specs: tuple
    out_shape: tuple
    scratch_shapes: tuple
    n_fill: int
    body: Callable


def matmul_ws(x, w, l, out_dtype, tm, tn, n_out, w_row=None, sq_relu=False, side=None, rider=None,
              name="matmul_ws"):
    m, kd = x.shape
    nj, ni = n_out // tn, m // tm
    assert m % tm == 0 and n_out % tn == 0 and ni >= 2
    w_shape = (kd, tn) if w_row is None else (tn, kd)
    in_specs = [pl.BlockSpec((tm, kd), lambda j, i, *_: (i, 0)), pl.BlockSpec(memory_space=pl.ANY)]
    out_specs = [pl.BlockSpec((tm, tn), lambda j, i, *_: (i, j))]
    out_shape = [jax.ShapeDtypeStruct((m, n_out), out_dtype)]
    args = [x, w]
    scratch = [pltpu.VMEM(w_shape, F32), pltpu.VMEM(w_shape, BF16), pltpu.SemaphoreType.DMA(())]
    if side is not None:
        _, sr, sc = side.shape
        assert sr % (nj * ni * 16) == 0
        slab = sr // (nj * ni)
        in_specs.append(pl.BlockSpec((None, slab, sc), lambda j, i, *_: (l, j * ni + i, 0)))
        out_specs.append(pl.BlockSpec((slab, sc), lambda j, i, *_: (j * ni + i, 0)))
        out_shape.append(jax.ShapeDtypeStruct((sr, sc), BF16))
        args.append(side)
    tables = ()
    if rider is not None:
        tables = rider.tables
        in_specs += list(rider.in_specs)
        out_specs += list(rider.out_specs)
        out_shape += list(rider.out_shape)
        args += list(rider.args)
        scratch += list(rider.scratch_shapes)
    out = pl.pallas_call(
        functools.partial(_mm_ws_kernel, l=l, tn=tn, nj=nj, w_row=w_row, sq_relu=sq_relu,
                          has_side=side is not None, rider=rider),
        grid_spec=pltpu.PrefetchScalarGridSpec(
            num_scalar_prefetch=len(tables), grid=(nj, ni), in_specs=in_specs, out_specs=out_specs,
            scratch_shapes=scratch),
        out_shape=out_shape,
        compiler_params=_cparams(("arbitrary", "arbitrary")),
        name=name,
    )(*tables, *args)
    return out[0] if len(out) == 1 else tuple(out)


def _split_bf16(x):
    hi = x.astype(BF16)
    lo = (x - hi.astype(F32)).astype(BF16)
    return hi, lo


LOG2E = 1.4426950408889634


def _log2_sigmoid_pair(z2):
    lo, hi = jnp.minimum(z2, 0.0), jnp.maximum(z2, 0.0)
    neg_sp = jnp.log(1.0 + jnp.exp2(lo - hi)) * (-LOG2E)
    return lo + neg_sp, neg_sp - hi


def _dot_nt(a, b):
    return lax.dot_general(a, b, (((1,), (1,)), ((), ())), preferred_element_type=F32)


def _dot_tn(a, b):
    return lax.dot_general(a, b, (((0,), (0,)), ((), ())), preferred_element_type=F32)


def _sb_pair_body(step, tables, ins, outs, scratch, fill, *, tq):
    _, q_tab, k_tab, _ = tables
    q_ref, k_ref, v_ref = ins
    (o_ref,) = outs
    run_ref, acc_ref = scratch
    qi, kj = q_tab[step], k_tab[step]

    @pl.when(step == 0)
    def _():
        run_ref[...] = jnp.zeros_like(run_ref)
        acc_ref[...] = jnp.zeros_like(acc_ref)

    dh = SB_DH
    scale2 = dh ** -0.5 * LOG2E
    nsub = tq // LANE
    rr = lax.broadcasted_iota(jnp.int32, (LANE, 2 * LANE), 0)
    cc = lax.broadcasted_iota(jnp.int32, (LANE, 2 * LANE), 1)
    cum_rhs = ((cc >= LANE) | (rr > cc)).astype(BF16)
    cum_rhs = jnp.concatenate([cum_rhs, cum_rhs], axis=0)
    rows = lax.broadcasted_iota(jnp.int32, (tq, tq), 0)
    cols = lax.broadcasted_iota(jnp.int32, (tq, tq), 1)
    strict = (cols - rows) < (qi - kj) * tq
    first = kj == qi
    for h in range(SB_HEADS):
        hs = slice(h * dh, (h + 1) * dh)
        run = jnp.where(first, 0.0, run_ref[h])
        acc = jnp.where(first, 0.0, acc_ref[h])
        z2 = _dot_nt(q_ref[:, hs], k_ref[:, hs]) * scale2
        fill()
        log_beta, log_1m = _log2_sigmoid_pair(z2)
        log_1m = jnp.where(strict, log_1m, 0.0)
        hi, lo = _split_bf16(log_1m)
        after = [None] * nsub
        for sb in reversed(range(nsub)):
            cs = slice(sb * LANE, (sb + 1) * LANE)
            r2 = jnp.dot(jnp.concatenate([hi[:, cs], lo[:, cs]], axis=1), cum_rhs, preferred_element_type=F32)
            after[sb] = r2[:, :LANE] + run
            run = run + r2[:, LANE:]
        fill()
        w = jnp.where(strict, jnp.exp2(log_beta + jnp.concatenate(after, axis=1)), 0.0)
        acc = acc + jnp.dot(w.astype(BF16), v_ref[:, hs], preferred_element_type=F32)
        run_ref[h] = run
        acc_ref[h] = acc
        o_ref[:, hs] = acc.astype(o_ref.dtype)


def stickbreaking_rider(p_sb, batch, seq, nj, ni, tq=SB_TQ):
    assert SB_DH == LANE
    nq = seq // tq
    items = [(b, qi, kj, 1) for b in range(batch) for qi in range(nq) for kj in range(qi, -1, -1)]
    assert len(items) <= nj * ni
    items += [items[-1][:3] + (0,)] * (nj * ni - len(items))
    tables = tuple(jnp.asarray([it[c] for it in items], jnp.int32) for c in range(4))
    wd = SB_HEADS * SB_DH

    def spec(tab, col):
        return pl.BlockSpec((tq, wd), lambda j, i, bt, qt, kt, work: (
            bt[j * ni + i] * nq + (qt, kt)[tab][j * ni + i], col))

    return Rider(
        tables=tables, args=(p_sb, p_sb, p_sb),
        in_specs=(spec(0, 0), spec(1, 1), spec(1, 2)),
        out_specs=(spec(0, 0),),
        out_shape=(jax.ShapeDtypeStruct((batch * seq, wd), BF16),),
        scratch_shapes=(pltpu.VMEM((SB_HEADS, tq, LANE), F32), pltpu.VMEM((SB_HEADS, tq, LANE), F32)),
        n_fill=2 * SB_HEADS, body=functools.partial(_sb_pair_body, tq=tq))


def _gla_kernel(q_ref, k_ref, v_ref, g_ref, small_ref, up_ref, bias_ref, gain_ref, o_ref, st_ref):
    c = GLA_CHUNK

    @pl.when(pl.program_id(1) == 0)
    def _():
        st_ref[...] = jnp.zeros_like(st_ref)

    r_i = lax.broadcasted_iota(jnp.int32, (c, c), 0)
    c_i = lax.broadcasted_iota(jnp.int32, (c, c), 1)
    tril = (c_i <= r_i).astype(BF16)
    sub = GLA_SUB
    lane = lax.broadcasted_iota(jnp.int32, (sub, LANE), 1)
    jrow = lax.broadcasted_iota(jnp.int32, (sub, LANE), 0)
    gain = gain_ref[...]
    u_hi, u_lo = _split_bf16(up_ref[...])
    states = [st_ref[h] for h in range(GLA_HEADS)]
    for cc in range(q_ref.shape[0] // c):
        rs = slice(cc * c, (cc + 1) * c)
        states = _gla_chunk(rs, states, q_ref, k_ref, v_ref, g_ref, small_ref, bias_ref, o_ref,
                            u_hi, u_lo, tril, lane, jrow, gain)
    for h in range(GLA_HEADS):
        st_ref[h] = states[h]


def _gla_chunk(rs, states, q_ref, k_ref, v_ref, g_ref, small_ref, bias_ref, o_ref, u_hi, u_lo, tril, lane, jrow,
               gain):
    c, sub = GLA_CHUNK, GLA_SUB
    a_hi, a_lo = _split_bf16(small_ref[rs, GA_LO:GA_LO + GLA_RANK])
    x = (jnp.dot(a_hi, u_hi, preferred_element_type=F32) + jnp.dot(a_hi, u_lo, preferred_element_type=F32)
         + jnp.dot(a_lo, u_hi, preferred_element_type=F32)) + bias_ref[...]
    log_a = _log2_sigmoid_pair(x * LOG2E)[0] / GLA_TAU
    g_hi, g_lo = _split_bf16(log_a)
    b_all = jnp.dot(tril, g_hi, preferred_element_type=F32) + jnp.dot(tril, g_lo, preferred_element_type=F32)

    new_states = []
    for h in range(GLA_HEADS):
        b = b_all[:, h * GLA_DK:(h + 1) * GLA_DK]
        q = q_ref[rs, h * GLA_DK:(h + 1) * GLA_DK] * (GLA_DK ** -0.5)
        k = k_ref[rs, h * GLA_DK:(h + 1) * GLA_DK]
        v = v_ref[rs, h * GLA_DV:(h + 1) * GLA_DV].astype(BF16)
        st = states[h]
        o_inter = _dot_nt((q * jnp.exp2(b)).astype(BF16), st.astype(BF16))

        o_rows = []
        for blk in range(c // sub):
            lo_r, hi_r = blk * sub, (blk + 1) * sub
            b_blk, q_blk, k_blk = b[lo_r:hi_r], q[lo_r:hi_r], k[lo_r:hi_r]
            o_blk = o_inter[lo_r:hi_r]
            if blk > 0:
                b_first = b[lo_r:lo_r + 1]
                qt = (q_blk * jnp.exp2(b_blk - b_first)).astype(BF16)
                kt = (k[:lo_r] * jnp.exp2(b_first - b[:lo_r])).astype(BF16)
                a_off = _dot_nt(qt, kt)
                o_blk = o_blk + jnp.dot(a_off.astype(BF16), v[:lo_r], preferred_element_type=F32)
            a_t = jnp.zeros((sub, LANE), F32)
            for i in range(sub):
                e = jnp.exp2(jnp.minimum(b_blk[i:i + 1] - b_blk, 0.0))
                col = jnp.sum(q_blk[i:i + 1] * k_blk * e, axis=-1, keepdims=True)
                a_t = jnp.where(lane == i, col, a_t)
            a_t = jnp.where(jrow <= lane, a_t, 0.0)
            o_diag = _dot_tn(a_t.astype(BF16), v[lo_r:hi_r])
            o_rows.append(o_blk + o_diag[:sub])
        o = jnp.concatenate(o_rows, axis=0)

        b_last = b[c - 1:c]
        kd = (k * jnp.exp2(b_last - b)).astype(BF16)
        new_states.append(st * jnp.exp2(b_last) + _dot_tn(v, kd))

        o = o * lax.rsqrt(jnp.mean(o * o, axis=-1, keepdims=True) + EPS) * gain
        g = g_ref[rs, h * GLA_DV:(h + 1) * GLA_DV]
        o = o * (g / (1.0 + jnp.exp(-g)))
        o_ref[rs, h * GLA_DV:(h + 1) * GLA_DV] = o.astype(o_ref.dtype)
    return new_states


def gla(p_gla, p_small, gate_up, gate_bias, head_gain, batch, seq):
    c = GLA_CHUNK * GLA_CHUNKS_PER_STEP
    n = seq // c
    wq, wv = GLA_HEADS * GLA_DK, GLA_HEADS * GLA_DV
    return pl.pallas_call(
        _gla_kernel,
        grid=(batch, n),
        in_specs=[pl.BlockSpec((c, wq), lambda b, i: (b * n + i, 0)),
                  pl.BlockSpec((c, wq), lambda b, i: (b * n + i, 1)),
                  pl.BlockSpec((c, wv), lambda b, i: (b * n + i, 1)),
                  pl.BlockSpec((c, wv), lambda b, i: (b * n + i, 2)),
                  pl.BlockSpec((c, LANE), lambda b, i: (b * n + i, 0)),
                  pl.BlockSpec((GLA_RANK, wq), lambda b, i: (0, 0)),
                  pl.BlockSpec((1, wq), lambda b, i: (0, 0)),
                  pl.BlockSpec((1, GLA_DV), lambda b, i: (0, 0))],
        out_specs=pl.BlockSpec((c, wv), lambda b, i: (b * n + i, 0)),
        out_shape=jax.ShapeDtypeStruct((batch * seq, wv), BF16),
        scratch_shapes=[pltpu.VMEM((GLA_HEADS, GLA_DV, GLA_DK), F32)],
        compiler_params=_cparams(("parallel", "arbitrary")),
        name="gla",
    )(p_gla, p_gla, p_gla, p_gla, p_small, gate_up, gate_bias.reshape(1, wq), head_gain.reshape(1, GLA_DV))


def _bias_kernel(bucket_ref, rel_ref, o_ref):
    for h in range(DSA_HEADS):
        for off in range(3):
            bk = bucket_ref[off]

            def body(n, acc, bk=bk, h=h):
                return jnp.where(bk == n, rel_ref[n, h], acc)

            o_ref[h, off] = lax.fori_loop(0, N_BUCKETS, body, jnp.zeros(bk.shape, F32)) * LOG2E


def _rel_bucket(dist):
    max_exact = N_BUCKETS // 2
    d = jnp.maximum(dist, 1).astype(F32)
    large = max_exact + (jnp.log(d / max_exact) / math.log(MAX_DISTANCE / max_exact)
                         * (N_BUCKETS - max_exact)).astype(jnp.int32)
    large = jnp.minimum(large, N_BUCKETS - 1)
    return jnp.where(dist < max_exact, dist, large)


def bias_tiles(rel_bias):
    s = jnp.arange(Q_BLOCK, dtype=jnp.int32)[:, None]
    t = jnp.arange(Q_BLOCK, dtype=jnp.int32)[None, :]
    dist = jnp.stack([off * Q_BLOCK + t - s for off in range(3)])
    bucket = _rel_bucket(jnp.maximum(dist, 0))
    return pl.pallas_call(
        _bias_kernel,
        in_specs=[pl.BlockSpec(memory_space=pltpu.VMEM), pl.BlockSpec(memory_space=pltpu.SMEM)],
        out_specs=pl.BlockSpec(memory_space=pltpu.VMEM),
        out_shape=jax.ShapeDtypeStruct((DSA_HEADS, 3, Q_BLOCK, Q_BLOCK), F32),
        name="bias_tiles",
    )(bucket, rel_bias)


DSA_CHUNK = 4 * Q_BLOCK
NEG_BIG = float(jnp.finfo(jnp.float32).min)


def _dsa_kernel(row_ref, smq_ref, sma_ref, dk_ref, dv_ref, bias_ref, o_ref,
                key_ref, neg_ref, q2_ref, acc_ref, *, topk):
    qb, ck = Q_BLOCK, DSA_CHUNK
    i = pl.program_id(1)
    bpc = ck // qb
    nck = (i + bpc) // bpc
    s_loc = lax.broadcasted_iota(jnp.int32, (ck, qb), 0)
    t_loc = lax.broadcasted_iota(jnp.int32, (ck, qb), 1)
    lane_k = lax.broadcasted_iota(jnp.int32, (1, LANE), 1)

    def visible(r0):
        return (r0 + s_loc) <= (i * qb + t_loc)

    def select_by_index():
        w_t = smq_ref[...].T * ((IDX_HEADS ** -0.5) * (IDX_DIM ** -0.5))
        for pr in range(IDX_HEADS // 2):
            qpair = row_ref[:, DSA_IQ + pr * LANE:DSA_IQ + (pr + 1) * LANE]
            zero = jnp.zeros_like(qpair)
            q2_ref[pr, :qb, :] = jnp.where(lane_k < IDX_DIM, qpair, zero)
            q2_ref[pr, qb:, :] = jnp.where(lane_k >= IDX_DIM, qpair, zero)

        def score_chunk(c, carry):
            r0 = pl.multiple_of(c * ck, ck)
            sm = sma_ref[pl.ds(r0, ck), :]
            k_dup = jnp.where(lane_k < IDX_DIM, pltpu.roll(sm, LANE - IK_LO, 1),
                              pltpu.roll(sm, IDX_DIM - IK_LO, 1)).astype(BF16)
            acc = jnp.zeros((ck, qb), F32)
            for pr in range(IDX_HEADS // 2):
                s2 = jnp.maximum(_dot_nt(k_dup, q2_ref[pr]), 0.0)
                he, ho = IW_LO + 2 * pr, IW_LO + 2 * pr + 1
                acc = acc + w_t[he:he + 1] * s2[:, :qb] + w_t[ho:ho + 1] * s2[:, qb:]
            val = jnp.where(visible(r0), acc + 0.0, -jnp.inf)
            bits = pltpu.bitcast(val, jnp.int32)
            key_ref[pl.ds(r0, ck), :] = jnp.where(bits < 0, bits ^ jnp.int32(0x7FFFFFFF), bits)
            return carry

        lax.fori_loop(0, nck, score_chunk, 0)

        kf = float(topk)

        def count(pred):
            def body(c, acc):
                r0 = pl.multiple_of(c * ck, ck)
                hit = jnp.where(pred(key_ref[pl.ds(r0, ck), :]), 1.0, 0.0)
                return acc + jnp.sum(hit.reshape(ck // 64, 64, qb), axis=0)
            acc = lax.fori_loop(0, nck, body, jnp.zeros((64, qb), F32))
            return jnp.sum(acc, axis=0, keepdims=True)

        int_min = jnp.int32(-2 ** 31)
        m0 = jnp.where(count(lambda kk: kk >= 0) >= kf, jnp.int32(0), int_min)

        def search(n, m):
            cand = m | lax.shift_left(jnp.int32(1), jnp.int32(30) - n)
            return jnp.where(count(lambda kk: kk >= cand) >= kf, cand, m)

        kth = lax.fori_loop(0, 31, search, m0)
        need = kf - count(lambda kk: kk > kth)

        tri_r = lax.broadcasted_iota(jnp.int32, (ck, ck), 0)
        tri_c = lax.broadcasted_iota(jnp.int32, (ck, ck), 1)
        tri = (tri_c <= tri_r).astype(BF16)

        def select_chunk(c, seen):
            r0 = pl.multiple_of(c * ck, ck)
            kblk = key_ref[pl.ds(r0, ck), :]
            eq = jnp.where(kblk == kth, 1.0, 0.0)
            rank = jnp.dot(tri, eq.astype(BF16), preferred_element_type=F32) + seen
            keep = jnp.where(kblk > kth, 1.0, jnp.where(rank <= need, eq, 0.0))
            keep = jnp.where(visible(r0), keep, 0.0)
            neg_ref[pl.ds(r0, ck), :] = jnp.where(keep > 0.0, 0.0, -jnp.inf)
            return seen + jnp.sum(eq, axis=0, keepdims=True)

        lax.fori_loop(0, nck, select_chunk, jnp.zeros((1, qb), F32))

    all_kept = (i + 1) * qb <= topk
    pl.when(jnp.logical_not(all_kept))(select_by_index)

    @pl.when(all_kept)
    def _():
        def body(c, carry):
            r0 = pl.multiple_of(c * ck, ck)
            neg_ref[pl.ds(r0, ck), :] = jnp.where(visible(r0), 0.0, -jnp.inf)
            return carry
        lax.fori_loop(0, nck, body, 0)

    acc_ref[...] = jnp.zeros_like(acc_ref)
    scale2 = DSA_DH ** -0.5 * LOG2E

    def attend_chunk(c, ml):
        r0 = pl.multiple_of(c * ck, ck)
        kc = dk_ref[pl.ds(r0, ck), :]
        vc = dv_ref[pl.ds(r0, ck), :]
        neg = neg_ref[pl.ds(r0, ck), :]
        offs = [jnp.clip(i - bpc * c - b, 0, 2) for b in range(bpc)]
        new_ml = []
        for h in range(DSA_HEADS):
            m_old, l_old = ml[h]
            bias = jnp.concatenate([bias_ref[h, off] for off in offs], axis=0)
            qh = row_ref[:, DSA_Q + h * DSA_DH:DSA_Q + (h + 1) * DSA_DH]
            lg = _dot_nt(kc, qh) * scale2 + bias + neg
            m_new = jnp.maximum(m_old, jnp.max(lg, axis=0, keepdims=True))
            alpha = jnp.exp2(m_old - m_new)
            pexp = jnp.exp2(lg - m_new)
            acc_ref[h] = alpha * acc_ref[h] + _dot_tn(vc, pexp.astype(BF16))
            new_ml.append((m_new, alpha * l_old + jnp.sum(pexp, axis=0, keepdims=True)))
        return tuple(new_ml)

    ml0 = tuple((jnp.full((1, qb), NEG_BIG, F32), jnp.zeros((1, qb), F32)) for _ in range(DSA_HEADS))
    ml = lax.fori_loop(0, nck, attend_chunk, ml0)
    for h in range(DSA_HEADS):
        o_t = acc_ref[h] / ml[h][1]
        o_ref[:, h * DSA_DH:(h + 1) * DSA_DH] = o_t.T.astype(o_ref.dtype)


def dsa(p_dsa, p_small, btiles, batch, seq):
    qb = Q_BLOCK
    nq = seq // qb
    assert seq % DSA_CHUNK == 0 and DSA_DH == LANE
    topk = min(TOPK_MAX, seq // 4)
    wq = DSA_HEADS * DSA_DH
    return pl.pallas_call(
        functools.partial(_dsa_kernel, topk=topk),
        grid=(batch, nq),
        in_specs=[pl.BlockSpec((qb, DSA_W), lambda b, i: (b * nq + i, 0)),
                  pl.BlockSpec((qb, LANE), lambda b, i: (b * nq + i, 1)),
                  pl.BlockSpec((seq, LANE), lambda b, i: (b, 1)),
                  pl.BlockSpec((seq, DSA_DH), lambda b, i: (b, DSA_K // DSA_DH)),
                  pl.BlockSpec((seq, DSA_DH), lambda b, i: (b, DSA_V // DSA_DH)),
                  pl.BlockSpec((DSA_HEADS, 3, qb, qb), lambda b, i: (0, 0, 0, 0))],
        out_specs=pl.BlockSpec((qb, wq), lambda b, i: (b * nq + i, 0)),
        out_shape=jax.ShapeDtypeStruct((batch * seq, wq), BF16),
        scratch_shapes=[pltpu.VMEM((seq, qb), jnp.int32), pltpu.VMEM((seq, qb), F32),
                        pltpu.VMEM((IDX_HEADS // 2, 2 * qb, LANE), BF16),
                        pltpu.VMEM((DSA_HEADS, DSA_DH, qb), F32)],
        compiler_params=_cparams(("parallel", "arbitrary")),
        name="dsa",
    )(p_dsa, p_small, p_small, p_dsa, p_dsa, btiles)


def _merge_kernel(oa_ref, ob_ref, oc_ref, ga_ref, gb_ref, gc_ref, w_ref, o_ref, *, tn):
    cols = pl.ds(pl.multiple_of(pl.program_id(1) * tn, LANE), tn)
    acc = None
    for br, (o_r, g_r) in enumerate(((oa_ref, ga_ref), (ob_ref, gb_ref), (oc_ref, gc_ref))):
        gate = 1.0 / (1.0 + jnp.exp(-g_r[...]))
        w = w_ref[br * BRANCH_WIDTH:(br + 1) * BRANCH_WIDTH, cols]
        term = gate * jnp.dot(o_r[...], w, preferred_element_type=F32)
        acc = term if acc is None else acc + term
    o_ref[...] = acc.astype(o_ref.dtype)


def gated_merge(o_a, o_b, o_c, p_gates, w_branch, tm=MERGE_TM, tn=MERGE_TN):
    m = o_a.shape[0]
    nj = D_MODEL // tn
    o_spec = pl.BlockSpec((tm, BRANCH_WIDTH), lambda i, j: (i, 0))

    def g_spec(br):
        return pl.BlockSpec((tm, tn), lambda i, j: (i, br * nj + j))

    w_spec = pl.BlockSpec(w_branch.shape, lambda i, j: (0, 0), pipeline_mode=pl.Buffered(1))
    return pl.pallas_call(
        functools.partial(_merge_kernel, tn=tn),
        grid=(m // tm, nj),
        in_specs=[o_spec, o_spec, o_spec, g_spec(0), g_spec(1), g_spec(2), w_spec],
        out_specs=pl.BlockSpec((tm, tn), lambda i, j: (i, j)),
        out_shape=jax.ShapeDtypeStruct((m, D_MODEL), BF16),
        compiler_params=_cparams(("parallel", "parallel")),
        name="gated_merge",
    )(o_a, o_b, o_c, p_gates, p_gates, p_gates, w_branch)


def mixer(h, x, l, btiles, norm_mix_post, norm_mlp_pre, w_in_t, gla_gate_up, gla_gate_bias, gla_head_gain,
          w_branch, w_out, batch, seq):
    def proj(src, out_dtype, name, tn=WS_TN, side=None, rider=None):
        off, width = src
        return matmul_ws(h, w_in_t, l, out_dtype, WS_TM, tn, width, w_row=lambda j: off + j * tn,
                         side=side, rider=rider, name=name)

    p_gla = proj(SRC_GLA, F32, "proj_gla")
    p_dsa = proj(SRC_DSA, BF16, "proj_dsa", tn=WS_TN // 2)
    p_sb = proj(SRC_SB, BF16, "proj_sb")
    sb = stickbreaking_rider(p_sb, batch, seq, SRC_GATES[1] // GATES_TN, h.shape[0] // WS_TM)
    p_gates, wb, o_c = proj(SRC_GATES, F32, "proj_gates_sb", tn=GATES_TN, rider=sb,
                            side=w_branch.reshape(DEPTH, N_BRANCHES * BRANCH_WIDTH, D_MODEL))
    p_small = matmul_ws(h, w_in_t, l, F32, WS_TM, LANE, 2 * LANE,
                        w_row=lambda j: SMALL_ROWS[0] + j * (SMALL_ROWS[1] - SMALL_ROWS[0]), name="proj_small")
    o_a = gla(p_gla, p_small, gla_gate_up[l], gla_gate_bias[l], gla_head_gain[l], batch, seq)
    o_b = dsa(p_dsa, p_small, btiles, batch, seq)
    merged = gated_merge(o_a, o_b, o_c, p_gates, wb)
    y = matmul_ws(merged, w_out, l, F32, WS_TM, WS_TN, D_MODEL, name="out_proj")
    return post_norm_residual(y, x, norm_mix_post[l], norm_mlp_pre[l])


def kernel(x, rel_bias, norm_mix_pre, norm_mix_post, norm_mlp_pre, norm_mlp_post, w_in, gla_gate_up,
           gla_gate_bias, gla_head_gain, w_branch, w_out, w_mlp_up, w_mlp_down):
    batch, seq, d = x.shape
    xf = x.reshape(batch * seq, d)
    btiles = bias_tiles(rel_bias)
    h = rmsnorm_cast(xf, norm_mix_pre[0])
    w_in_t = jnp.swapaxes(w_in, 1, 2)
    for l in range(DEPTH):
        xf, h = mixer(h, xf, l, btiles, norm_mix_post, norm_mlp_pre, w_in_t, gla_gate_up, gla_gate_bias,
                      gla_head_gain, w_branch, w_out, batch, seq)
        u, w_down = matmul_ws(h, w_mlp_up, l, BF16, WS_TM, WS_TN, D_FF, sq_relu=True, side=w_mlp_down,
                              name="mlp_up")
        y = matmul_ktiled(u, w_down, F32, DOWN_TM, DOWN_TN, DOWN_TK, name="mlp_down")
        g_next = norm_mix_pre[l + 1] if l + 1 < DEPTH else None
        xf, h = post_norm_residual(y, xf, norm_mlp_post[l], g_next)
    return xf.reshape(batch, seq, d)
```
